```python
import math
import jax
import jax.numpy as jnp
from jax import lax
import numpy as np

D_MODEL = 4096
BATCH = 2
SEQ = 8192
DEPTH = 4

N_MIXERS = 3
DN_ALPHA = (2 * DEPTH) ** 0.25
DN_BETA = (8 * DEPTH) ** -0.25
LN_EPS = 1e-5
CONV_K = 4

GDN_DK = 128
GDN_DV = 128
GDN_HEADS = D_MODEL // GDN_DK
GDN_QK = GDN_HEADS * GDN_DK
GDN_V = GDN_HEADS * GDN_DV
GDN_CHUNK = 64

RWKV_HEAD = 64
RWKV_HEADS = D_MODEL // RWKV_HEAD
RWKV_DECAY_LORA = max(32, int(round(1.8 * D_MODEL ** 0.5 / 32)) * 32)
RWKV_A_LORA = max(32, int(round(1.8 * D_MODEL ** 0.5 / 32)) * 32)
RWKV_GATE_LORA = max(32, int(round(0.6 * D_MODEL ** 0.8 / 32)) * 32)
RWKV_GN_EPS = 64e-5

LRU_WIDTH = D_MODEL
LRU_BLOCK = 256
LRU_BLOCKS = LRU_WIDTH // LRU_BLOCK
LRU_C = 8.0

FFN_DENSE = 2 * D_MODEL
N_EXPERTS = 8
TOP_K = 2
FFN_EXPERT = 3 * D_MODEL // 8

N_GDN = (DEPTH + 2) // 3
N_RWKV = (DEPTH + 1) // 3
N_LRU = DEPTH // 3
N_DENSE = (DEPTH + 1) // 2
N_MOE = DEPTH // 2

kernel_name = "hybrid_gdn_rwkv7_rglru_moe_deepnorm"


def layer_norm(x, g, b):
    xf = x.astype(jnp.float32)
    mu = jnp.mean(xf, -1, keepdims=True)
    var = jnp.mean(jnp.square(xf - mu), -1, keepdims=True)
    return ((xf - mu) * lax.rsqrt(var + LN_EPS) * g + b).astype(x.dtype)


def l2_normalize(x, eps=1e-6):
    return x * lax.rsqrt(jnp.sum(x * x, -1, keepdims=True) + eps)


def causal_dwconv(x, w):
    K = w.shape[0]
    T = x.shape[1]
    xp = jnp.pad(x, ((0, 0), (K - 1, 0), (0, 0)))
    y = xp[:, 0:T] * w[0]
    for j in range(1, K):
        y = y + xp[:, j:j + T] * w[j]
    return y


def gated_delta_chunked(q, k, v, g, beta):
    B, T, H, DK = q.shape
    DV = v.shape[-1]
    C = GDN_CHUNK
    N = T // C

    def chunks(t):
        t = t.reshape((B, N, C, H) + t.shape[3:])
        return jnp.moveaxis(t, (1, 3), (0, 2))

    qc, kc, vc = chunks(q), chunks(k), chunks(v)
    gc, bc = chunks(g), chunks(beta)
    G = jnp.cumsum(gc, axis=-1)
    causal = jnp.tril(jnp.ones((C, C), dtype=bool))
    strict = jnp.tril(jnp.ones((C, C), dtype=bool), -1)
    diff = G[..., :, None] - G[..., None, :]
    gamma = jnp.where(causal, jnp.exp(jnp.where(causal, diff, 0.0)), 0.0)
    kb = kc * bc[..., None]
    A = jnp.where(strict, jnp.einsum('nbhcd,nbhsd->nbhcs', kb, kc) * gamma, 0.0)
    eye = jnp.eye(C, dtype=A.dtype)
    Tinv = lax.linalg.triangular_solve(A + eye, jnp.broadcast_to(eye, A.shape),
                                       left_side=True, lower=True, unit_diagonal=True)
    u = jnp.einsum('nbhcs,nbhse->nbhce', Tinv, vc * bc[..., None])
    w = jnp.einsum('nbhcs,nbhsd->nbhcd', Tinv, kb * jnp.exp(G)[..., None])
    qk = jnp.einsum('nbhcd,nbhsd->nbhcs', qc, kc) * gamma

    def step(S, inp):
        q_i, k_i, u_i, w_i, G_i, qk_i = inp
        v_new = u_i - jnp.einsum('bhcd,bhde->bhce', w_i, S)
        o = (jnp.einsum('bhcd,bhde->bhce', q_i * jnp.exp(G_i)[..., None], S)
             + jnp.einsum('bhcs,bhse->bhce', qk_i, v_new))
        G_last = G_i[..., -1:]
        S = (S * jnp.exp(G_last)[..., None]
             + jnp.einsum('bhcd,bhce->bhde', k_i * jnp.exp(G_last - G_i)[..., None], v_new))
        return S, o

    S0 = jnp.zeros((B, H, DK, DV), q.dtype)
    _, o = lax.scan(step, S0, (qc, kc, u, w, G, qk))
    o = jnp.moveaxis(o, (0, 2), (1, 3))
    return o.reshape(B, T, H, DV)


def gdn_mixer(x, w_in, conv_w, a_log, dt_bias, norm_w, w_out):
    B, T, _ = x.shape
    f32 = jnp.float32
    proj = x @ w_in
    qkv, z, b, a = jnp.split(proj, [2 * GDN_QK + GDN_V, 2 * GDN_QK + 2 * GDN_V,
                                    2 * GDN_QK + 2 * GDN_V + GDN_HEADS], axis=-1)
    qkv = jax.nn.silu(causal_dwconv(qkv, conv_w)).astype(f32)
    q, k, v = jnp.split(qkv, [GDN_QK, 2 * GDN_QK], axis=-1)
    q = l2_normalize(q.reshape(B, T, GDN_HEADS, GDN_DK)) * (GDN_DK ** -0.5)
    k = l2_normalize(k.reshape(B, T, GDN_HEADS, GDN_DK))
    v = v.reshape(B, T, GDN_HEADS, GDN_DV)
    beta = jax.nn.sigmoid(b.astype(f32))
    g = -jnp.exp(a_log.astype(f32)) * jax.nn.softplus(a.astype(f32) + dt_bias.astype(f32))
    o = gated_delta_chunked(q, k, v, g, beta)
    o = o * lax.rsqrt(jnp.mean(o * o, -1, keepdims=True) + 1e-6) * norm_w
    o = o * jax.nn.silu(z.reshape(B, T, GDN_HEADS, GDN_DV).astype(f32))
    return o.reshape(B, T, GDN_V).astype(x.dtype) @ w_out


def rwkv7_mixer(x, mu, w_rkv, w0, w1, w2, a0, a1, a2, g1, g2, k_k, k_a, r_k, gn_g, gn_b, w_out):
    B, T, D = x.shape
    H, N = RWKV_HEADS, RWKV_HEAD
    f32 = jnp.float32
    xx = jnp.pad(x, ((0, 0), (1, 0), (0, 0)))[:, :T] - x
    x_rkv = x[None] + xx[None] * mu[:3, None, None, :]
    rkv = jnp.einsum('pbtd,pde->pbte', x_rkv, w_rkv).astype(f32)
    r, k, v = rkv[0], rkv[1], rkv[2]
    xw = x + xx * mu[3]
    xa = x + xx * mu[4]
    xg = x + xx * mu[5]
    w = -jax.nn.softplus(-(w0 + jnp.tanh(xw @ w1) @ w2).astype(f32)) - 0.5
    decay = jnp.exp(-jnp.exp(w))
    a = jax.nn.sigmoid((a0 + (xa @ a1) @ a2).astype(f32))
    g = (jax.nn.sigmoid(xg @ g1) @ g2).astype(f32)

    def heads(t):
        return t.reshape(B, T, H, N)

    def time_major(t):
        return jnp.moveaxis(t, 1, 0)

    kk = l2_normalize(heads(k * k_k))
    k = k * (1.0 + (a - 1.0) * k_a)
    r, k, v, a, decay = heads(r), heads(k), heads(v), heads(a), heads(decay)

    def step(S, inp):
        r_t, w_t, k_t, v_t, kk_t, kka_t = inp
        sa = jnp.einsum('bhvk,bhk->bhv', S, kk_t)
        S = (S * w_t[:, :, None, :] - sa[..., None] * kka_t[:, :, None, :]
             + v_t[..., None] * k_t[:, :, None, :])
        return S, jnp.einsum('bhvk,bhk->bhv', S, r_t)

    S0 = jnp.zeros((B, H, N, N), f32)
    _, y = lax.scan(step, S0, (time_major(r), time_major(decay), time_major(k),
                               time_major(v), time_major(kk), time_major(kk * a)))
    y = jnp.moveaxis(y, 0, 1)
    mean = jnp.mean(y, -1, keepdims=True)
    var = jnp.mean(jnp.square(y - mean), -1, keepdims=True)
    y = ((y - mean) * lax.rsqrt(var + RWKV_GN_EPS)).reshape(B, T, D) * gn_g + gn_b
    bonus = jnp.sum(r * k * r_k.reshape(H, N), -1, keepdims=True) * v
    y = y + bonus.reshape(B, T, D)
    return (y * g).astype(x.dtype) @ w_out


def rglru_mixer(x, w_in, conv_w, conv_b, w_gx, b_gx, w_ga, b_ga, lam, w_out):
    B, T, _ = x.shape
    f32 = jnp.float32
    gate_br, rec = jnp.split(x @ w_in, 2, axis=-1)
    gate = jax.nn.gelu(gate_br, approximate=True)
    u = causal_dwconv(rec, conv_w) + conv_b
    ub = u.reshape(B, T, LRU_BLOCKS, LRU_BLOCK)
    i_t = jax.nn.sigmoid(jnp.einsum('bthi,hij->bthj', ub, w_gx).reshape(B, T, LRU_WIDTH) + b_gx)
    r_t = jax.nn.sigmoid(jnp.einsum('bthi,hij->bthj', ub, w_ga).reshape(B, T, LRU_WIDTH) + b_ga)
    log_a = -LRU_C * r_t.astype(f32) * jax.nn.softplus(-lam.astype(f32))
    a = jnp.exp(log_a)
    b = jnp.sqrt(-jnp.expm1(2.0 * log_a)) * (i_t * u).astype(f32)

    def combine(p, q):
        a1, b1 = p
        a2, b2 = q
        return a1 * a2, a2 * b1 + b2

    _, h = lax.associative_scan(combine, (a, b), axis=1)
    return (h.astype(x.dtype) * gate) @ w_out


def swiglu(x, w_gu, w_down):
    g, u = jnp.split(x @ w_gu, 2, axis=-1)
    return (jax.nn.silu(g) * u) @ w_down


def moe_swiglu(x, router_w, router_b, w_gu, w_down):
    B, T, D = x.shape
    xt = x.reshape(B * T, D)
    logits = (xt @ router_w).astype(jnp.float32) + router_b.astype(jnp.float32)
    top_v, top_i = lax.top_k(logits, TOP_K)
    top_p = jax.nn.softmax(top_v, axis=-1)
    comb = jnp.sum(jax.nn.one_hot(top_i, N_EXPERTS, dtype=jnp.float32) * top_p[..., None], axis=1)
    comb = comb.astype(x.dtype)
    y = jnp.zeros_like(xt)
    for e in range(N_EXPERTS):
        y = y + comb[:, e:e + 1] * swiglu(xt, w_gu[e], w_down[e])
    return y.reshape(B, T, D)


def setup_inputs(seed: int = 0) -> dict:
    key = jax.random.key(seed)
    ks = iter(jax.random.split(key, 64))
    f32 = jnp.float32
    D = D_MODEL
    R = LRU_WIDTH

    def nrm(shape, scale):
        return jax.random.normal(next(ks), shape, f32) * scale

    def uni(shape, lo, hi):
        return jax.random.uniform(next(ks), shape, f32, lo, hi)

    x = nrm((BATCH, SEQ, D), 1.0)
    ln_g = 1.0 + nrm((DEPTH, 2, D), 0.02)
    ln_b = nrm((DEPTH, 2, D), 0.02)
    gdn_w_in = nrm((N_GDN, D, 2 * GDN_QK + 2 * GDN_V + 2 * GDN_HEADS), D ** -0.5)
    gdn_conv_w = nrm((N_GDN, CONV_K, 2 * GDN_QK + GDN_V), CONV_K ** -0.5)
    gdn_a_log = jnp.log(uni((N_GDN, GDN_HEADS), 1.0, 16.0))
    dt = jnp.exp(uni((N_GDN, GDN_HEADS), math.log(1e-3), math.log(1e-1)))
    gdn_dt_bias = dt + jnp.log(-jnp.expm1(-dt))
    gdn_norm_w = 1.0 + nrm((N_GDN, GDN_DV), 0.02)
    gdn_w_out = nrm((N_GDN, GDN_V, D), GDN_V ** -0.5 * DN_BETA)
    rwkv_mu = uni((N_RWKV, 6, D), 0.0, 1.0)
    rwkv_w_rkv = nrm((N_RWKV, 3, D, D), D ** -0.5)
    ratio = jnp.arange(D, dtype=f32) / (D - 1)
    rwkv_w0 = (-6.5 + 5.0 * ratio ** 0.85)[None] + nrm((N_RWKV, D), 0.1)
    rwkv_w1 = nrm((N_RWKV, D, RWKV_DECAY_LORA), D ** -0.5)
    rwkv_w2 = nrm((N_RWKV, RWKV_DECAY_LORA, D), 0.1 * RWKV_DECAY_LORA ** -0.5)
    rwkv_a0 = nrm((N_RWKV, D), 0.1)
    rwkv_a1 = nrm((N_RWKV, D, RWKV_A_LORA), D ** -0.5)
    rwkv_a2 = nrm((N_RWKV, RWKV_A_LORA, D), RWKV_A_LORA ** -0.5)
    rwkv_g1 = nrm((N_RWKV, D, RWKV_GATE_LORA), D ** -0.5)
    rwkv_g2 = nrm((N_RWKV, RWKV_GATE_LORA, D), RWKV_GATE_LORA ** -0.5)
    rwkv_k_k = 0.85 + nrm((N_RWKV, D), 0.02)
    rwkv_k_a = 1.0 + nrm((N_RWKV, D), 0.02)
    rwkv_r_k = nrm((N_RWKV, D), 0.1)
    rwkv_gn_g = 1.0 + nrm((N_RWKV, D), 0.02)
    rwkv_gn_b = nrm((N_RWKV, D), 0.02)
    rwkv_w_out = nrm((N_RWKV, D, D), D ** -0.5 * DN_BETA)
    lru_w_in = nrm((N_LRU, D, 2 * R), D ** -0.5)
    lru_conv_w = nrm((N_LRU, CONV_K, R), CONV_K ** -0.5)
    lru_conv_b = nrm((N_LRU, R), 0.02)
    lru_w_gx = nrm((N_LRU, LRU_BLOCKS, LRU_BLOCK, LRU_BLOCK), LRU_BLOCK ** -0.5)
    lru_b_gx = nrm((N_LRU, R), 0.02)
    lru_w_ga = nrm((N_LRU, LRU_BLOCKS, LRU_BLOCK, LRU_BLOCK), LRU_BLOCK ** -0.5)
    lru_b_ga = nrm((N_LRU, R), 0.02)
    s = uni((N_LRU, R), 0.9, 0.999) ** (1.0 / LRU_C)
    lru_lam = jnp.log(s) - jnp.log1p(-s)
    lru_w_out = nrm((N_LRU, R, D), R ** -0.5 * DN_BETA)
    ffn_w_gu = nrm((N_DENSE, D, 2 * FFN_DENSE), D ** -0.5)
    ffn_w_down = nrm((N_DENSE, FFN_DENSE, D), FFN_DENSE ** -0.5 * DN_BETA)
    moe_router_w = nrm((N_MOE, D, N_EXPERTS), D ** -0.5)
    moe_router_b = nrm((N_MOE, N_EXPERTS), 0.01)
    moe_w_gu = nrm((N_MOE, N_EXPERTS, D, 2 * FFN_EXPERT), D ** -0.5)
    moe_w_down = nrm((N_MOE, N_EXPERTS, FFN_EXPERT, D), FFN_EXPERT ** -0.5 * DN_BETA)
    return {
        "x": x, "ln_g": ln_g, "ln_b": ln_b,
        "gdn_w_in": gdn_w_in, "gdn_conv_w": gdn_conv_w, "gdn_a_log": gdn_a_log,
        "gdn_dt_bias": gdn_dt_bias, "gdn_norm_w": gdn_norm_w, "gdn_w_out": gdn_w_out,
        "rwkv_mu": rwkv_mu, "rwkv_w_rkv": rwkv_w_rkv, "rwkv_w0": rwkv_w0, "rwkv_w1": rwkv_w1,
        "rwkv_w2": rwkv_w2, "rwkv_a0": rwkv_a0, "rwkv_a1": rwkv_a1, "rwkv_a2": rwkv_a2,
        "rwkv_g1": rwkv_g1, "rwkv_g2": rwkv_g2, "rwkv_k_k": rwkv_k_k, "rwkv_k_a": rwkv_k_a,
        "rwkv_r_k": rwkv_r_k, "rwkv_gn_g": rwkv_gn_g, "rwkv_gn_b": rwkv_gn_b, "rwkv_w_out": rwkv_w_out,
        "lru_w_in": lru_w_in, "lru_conv_w": lru_conv_w, "lru_conv_b": lru_conv_b,
        "lru_w_gx": lru_w_gx, "lru_b_gx": lru_b_gx, "lru_w_ga": lru_w_ga, "lru_b_ga": lru_b_ga,
        "lru_lam": lru_lam, "lru_w_out": lru_w_out,
        "ffn_w_gu": ffn_w_gu, "ffn_w_down": ffn_w_down,
        "moe_router_w": moe_router_w, "moe_router_b": moe_router_b,
        "moe_w_gu": moe_w_gu, "moe_w_down": moe_w_down,
    }


def reference(x, ln_g, ln_b,
              gdn_w_in, gdn_conv_w, gdn_a_log, gdn_dt_bias, gdn_norm_w, gdn_w_out,
              rwkv_mu, rwkv_w_rkv, rwkv_w0, rwkv_w1, rwkv_w2, rwkv_a0, rwkv_a1, rwkv_a2,
              rwkv_g1, rwkv_g2, rwkv_k_k, rwkv_k_a, rwkv_r_k, rwkv_gn_g, rwkv_gn_b, rwkv_w_out,
              lru_w_in, lru_conv_w, lru_conv_b, lru_w_gx, lru_b_gx, lru_w_ga, lru_b_ga,
              lru_lam, lru_w_out,
              ffn_w_gu, ffn_w_down,
              moe_router_w, moe_router_b, moe_w_gu, moe_w_down):
    h = x
    for i in range(DEPTH):
        m, s = i % N_MIXERS, i // N_MIXERS
        if m == 0:
            y = gdn_mixer(h, gdn_w_in[s], gdn_conv_w[s], gdn_a_log[s], gdn_dt_bias[s],
                          gdn_norm_w[s], gdn_w_out[s])
        elif m == 1:
            y = rwkv7_mixer(h, rwkv_mu[s], rwkv_w_rkv[s], rwkv_w0[s], rwkv_w1[s], rwkv_w2[s],
                            rwkv_a0[s], rwkv_a1[s], rwkv_a2[s], rwkv_g1[s], rwkv_g2[s],
                            rwkv_k_k[s], rwkv_k_a[s], rwkv_r_k[s], rwkv_gn_g[s], rwkv_gn_b[s],
                            rwkv_w_out[s])
        else:
            y = rglru_mixer(h, lru_w_in[s], lru_conv_w[s], lru_conv_b[s], lru_w_gx[s], lru_b_gx[s],
                            lru_w_ga[s], lru_b_ga[s], lru_lam[s], lru_w_out[s])
        h = layer_norm(DN_ALPHA * h + y, ln_g[i, 0], ln_b[i, 0])
        f = i // 2
        if i % 2 == 0:
            y = swiglu(h, ffn_w_gu[f], ffn_w_down[f])
        else:
            y = moe_swiglu(h, moe_router_w[f], moe_router_b[f], moe_w_gu[f], moe_w_down[f])
        h = layer_norm(DN_ALPHA * h + y, ln_g[i, 1], ln_b[i, 1])
    return h
```

```python
import functools
import math

import jax
import jax.numpy as jnp
from jax import lax
from jax.experimental import pallas as pl
from jax.experimental.pallas import tpu as pltpu

F32 = jnp.float32
BF16 = jnp.bfloat16

D_MODEL = 4096
DEPTH = 4
DN_ALPHA = (2 * DEPTH) ** 0.25
LN_EPS = 1e-5
CONV_K = 4
GDN_DK = 128
GDN_HEADS = D_MODEL // GDN_DK
RWKV_HEAD = 64
RWKV_GN_EPS = 64e-5
LRU_BLOCK = 256
LRU_BLOCKS = D_MODEL // LRU_BLOCK
LRU_C = 8.0
N_EXPERTS = 8

V7X_VMEM_BYTES = 64 * 1024 * 1024
VMEM_LIMIT = V7X_VMEM_BYTES - 8 * 1024 * 1024
LANES = 128
SUBLANES = 8

GDN_CHUNK = 256
RWKV_CHUNK = 64
LRU_TBLOCK = 256
INV_BASE = 16


def _params(sem):
    return pltpu.CompilerParams(dimension_semantics=sem, vmem_limit_bytes=VMEM_LIMIT)


def _dot(a, b):
    return jnp.dot(a.astype(BF16), b.astype(BF16), preferred_element_type=F32)


def _dot_nt(a, b):
    return lax.dot_general(a.astype(BF16), b.astype(BF16), (((1,), (1,)), ((), ())),
                           preferred_element_type=F32)


def _dot_tn(a, b):
    return lax.dot_general(a.astype(BF16), b.astype(BF16), (((0,), (0,)), ((), ())),
                           preferred_element_type=F32)


def _sigmoid(x):
    return 1.0 / (1.0 + jnp.exp(-x))


def _silu(x):
    return x * _sigmoid(x)


def _softplus(x):
    return jnp.maximum(x, 0.0) + jnp.log1p(jnp.exp(-jnp.abs(x)))


def _pick(n, pref):
    t = min(n, pref)
    while n % t:
        t //= 2
    return t


def _mm_kernel(x_ref, w_ref, o_ref, *, nk):
    acc = jnp.dot(x_ref[...], w_ref[...], preferred_element_type=F32)
    if nk == 1:
        o_ref[...] = acc.astype(o_ref.dtype)
    else:
        k = pl.program_id(2)

        @pl.when(k == 0)
        def _():
            o_ref[...] = acc

        @pl.when(k > 0)
        def _():
            o_ref[...] += acc


def matmul(x, w, out_dtype, bm=1024, bn=1024, bk=4096):
    m, kdim = x.shape
    n = w.shape[1]
    bm, bn, bk = _pick(m, bm), _pick(n, bn), _pick(kdim, bk)
    nk = kdim // bk
    if nk > 1:
        assert out_dtype == F32
    return pl.pallas_call(
        functools.partial(_mm_kernel, nk=nk),
        out_shape=jax.ShapeDtypeStruct((m, n), out_dtype),
        grid=(m // bm, n // bn, nk),
        in_specs=[pl.BlockSpec((bm, bk), lambda i, j, k: (i, k)),
                  pl.BlockSpec((bk, bn), lambda i, j, k: (k, j))],
        out_specs=pl.BlockSpec((bm, bn), lambda i, j, k: (i, j)),
        compiler_params=_params(("parallel", "parallel", "arbitrary")),
        name="matmul",
    )(x, w)


def _swiglu_kernel(x_ref, wg_ref, wu_ref, o_ref):
    x = x_ref[...]
    g = jnp.dot(x, wg_ref[...], preferred_element_type=F32)
    u = jnp.dot(x, wu_ref[...], preferred_element_type=F32)
    o_ref[...] = (_silu(g) * u).astype(o_ref.dtype)


def swiglu_up(x, w_gu, bm=1024, bn=512):
    m, d = x.shape
    f = w_gu.shape[1] // 2
    bm, bn = _pick(m, bm), _pick(f, bn)
    nj = f // bn
    return pl.pallas_call(
        _swiglu_kernel,
        out_shape=jax.ShapeDtypeStruct((m, f), BF16),
        grid=(m // bm, nj),
        in_specs=[pl.BlockSpec((bm, d), lambda i, j: (i, 0)),
                  pl.BlockSpec((d, bn), lambda i, j: (0, j)),
                  pl.BlockSpec((d, bn), lambda i, j: (0, j + nj))],
        out_specs=pl.BlockSpec((bm, bn), lambda i, j: (i, j)),
        compiler_params=_params(("parallel", "parallel")),
        name="swiglu_up",
    )(x, w_gu, w_gu)


def _moe_up_kernel(x_ref, wg_ref, wu_ref, comb_ref, o_ref):
    e = pl.program_id(1)
    x = x_ref[...]
    g = jnp.dot(x, wg_ref[...], preferred_element_type=F32)
    u = jnp.dot(x, wu_ref[...], preferred_element_type=F32)
    comb = comb_ref[...]
    lane = lax.broadcasted_iota(jnp.int32, comb.shape, 1)
    ce = jnp.sum(jnp.where(lane == e, comb, 0.0), axis=1, keepdims=True)
    o_ref[...] = (_silu(g) * u * ce).astype(o_ref.dtype)


def moe_up(x, w_gu, comb, bm=1024, bn=512):
    m, d = x.shape
    ne = w_gu.shape[0]
    f = w_gu.shape[2] // 2
    bm, bn = _pick(m, bm), _pick(f, bn)
    nj = f // bn
    return pl.pallas_call(
        _moe_up_kernel,
        out_shape=jax.ShapeDtypeStruct((m, ne * f), BF16),
        grid=(m // bm, ne, nj),
        in_specs=[pl.BlockSpec((bm, d), lambda i, e, j: (i, 0)),
                  pl.BlockSpec((None, d, bn), lambda i, e, j: (e, 0, j)),
                  pl.BlockSpec((None, d, bn), lambda i, e, j: (e, 0, j + nj)),
                  pl.BlockSpec((bm, LANES), lambda i, e, j: (i, 0))],
        out_specs=pl.BlockSpec((bm, bn), lambda i, e, j: (i, e * nj + j)),
        compiler_params=_params(("parallel", "parallel", "parallel")),
        name="moe_up",
    )(x, w_gu, w_gu, comb)


def _router_kernel(x_ref, w_ref, b_ref, o_ref):
    logits = jnp.dot(x_ref[...], w_ref[...], preferred_element_type=F32,
                     precision=lax.Precision.HIGHEST) + b_ref[...]
    lane = lax.broadcasted_iota(jnp.int32, logits.shape, 1)
    neg = jnp.float32(-jnp.inf)
    logits = jnp.where(lane < N_EXPERTS, logits, neg)
    m1 = jnp.max(logits, axis=1, keepdims=True)
    i1 = jnp.min(jnp.where(logits == m1, lane, LANES), axis=1, keepdims=True)
    sel1 = lane == i1
    rest = jnp.where(sel1, neg, logits)
    m2 = jnp.max(rest, axis=1, keepdims=True)
    i2 = jnp.min(jnp.where(rest == m2, lane, LANES), axis=1, keepdims=True)
    sel2 = lane == i2
    e2 = jnp.exp(m2 - m1)
    p1 = 1.0 / (1.0 + e2)
    p2 = e2 / (1.0 + e2)
    o_ref[...] = jnp.where(sel1, p1, 0.0) + jnp.where(sel2, p2, 0.0)


def router(x, rw, rb, bm=512):
    m, d = x.shape
    bm = _pick(m, bm)
    rw_p = jnp.zeros((d, LANES), F32).at[:, :N_EXPERTS].set(rw.astype(F32))
    rb_p = jnp.zeros((1, LANES), F32).at[0, :N_EXPERTS].set(rb.astype(F32))
    return pl.pallas_call(
        _router_kernel,
        out_shape=jax.ShapeDtypeStruct((m, LANES), F32),
        grid=(m // bm,),
        in_specs=[pl.BlockSpec((bm, d), lambda i: (i, 0)),
                  pl.BlockSpec((d, LANES), lambda i: (0, 0)),
                  pl.BlockSpec((1, LANES), lambda i: (0, 0))],
        out_specs=pl.BlockSpec((bm, LANES), lambda i: (i, 0)),
        compiler_params=_params(("parallel",)),
        name="router",
    )(x, rw_p, rb_p)


def _ln_kernel(h_ref, y_ref, g_ref, b_ref, o_ref, ob_ref):
    x = DN_ALPHA * h_ref[...] + y_ref[...]
    mu = jnp.mean(x, axis=-1, keepdims=True)
    xc = x - mu
    var = jnp.mean(xc * xc, axis=-1, keepdims=True)
    o = xc * lax.rsqrt(var + LN_EPS) * g_ref[...] + b_ref[...]
    o_ref[...] = o
    ob_ref[...] = o.astype(BF16)


def ln_residual(h, y, g, b, tm=256):
    m, d = h.shape
    tm = _pick(m, tm)
    row = pl.BlockSpec((tm, d), lambda i: (i, 0))
    vec = pl.BlockSpec((1, d), lambda i: (0, 0))
    return pl.pallas_call(
        _ln_kernel,
        out_shape=(jax.ShapeDtypeStruct((m, d), F32), jax.ShapeDtypeStruct((m, d), BF16)),
        grid=(m // tm,),
        in_specs=[row, row, vec, vec],
        out_specs=(row, row),
        compiler_params=_params(("parallel",)),
        name="ln_residual",
    )(h, y, g.reshape(1, d), b.reshape(1, d))


def _causal_conv(x, tail, w):
    c = x.shape[0]
    xs = jnp.concatenate([tail, x], axis=0)
    y = xs[SUBLANES:] * w[CONV_K - 1:CONV_K]
    for j in range(CONV_K - 1):
        y = y + pltpu.roll(xs, CONV_K - 1 - j, axis=0)[SUBLANES:SUBLANES + c] * w[j:j + 1]
    return y


def _unit_lower_inverse(a_strict, top):
    c = a_strict.shape[0]
    ii = lax.broadcasted_iota(jnp.int32, (c, c), 0)
    jj = lax.broadcasted_iota(jnp.int32, (c, c), 1)
    sh = INV_BASE.bit_length() - 1
    p = jnp.where((ii >> sh) == (jj >> sh), -a_strict, 0.0)
    x = jnp.where(ii == jj, 1.0, 0.0) + p
    k = 2
    while k < INV_BASE:
        p = _dot(p, p)
        x = x + _dot(x, p)
        k *= 2
    size = INV_BASE
    while size < top:
        sh = size.bit_length() - 1
        bi, bj = ii >> sh, jj >> sh
        off = ((bi >> 1) == (bj >> 1)) & ((bi & 1) == 1) & ((bj & 1) == 0)
        x = x - _dot(x, _dot(jnp.where(off, a_strict, 0.0), x))
        size *= 2
    return x


def _gdn_kernel(alog_ref, dtb_ref, q_ref, k_ref, v_ref, z_ref, ba_ref,
                cwq_ref, cwk_ref, cwv_ref, nw_ref, o_ref, s_ref, tail_ref):
    h = pl.program_id(1)
    t = pl.program_id(2)
    c = q_ref.shape[0]

    @pl.when(t == 0)
    def _():
        s_ref[...] = jnp.zeros_like(s_ref)
        tail_ref[...] = jnp.zeros_like(tail_ref)

    def conv_silu(x_ref, cw_ref, slot):
        x = x_ref[...].astype(F32)
        y = _causal_conv(x, tail_ref[slot], cw_ref[...])
        tail_ref[slot] = x[c - SUBLANES:]
        return _silu(y)

    q = conv_silu(q_ref, cwq_ref, 0)
    k = conv_silu(k_ref, cwk_ref, 1)
    v = conv_silu(v_ref, cwv_ref, 2)
    q = q * lax.rsqrt(jnp.sum(q * q, axis=-1, keepdims=True) + 1e-6) * (GDN_DK ** -0.5)
    k = k * lax.rsqrt(jnp.sum(k * k, axis=-1, keepdims=True) + 1e-6)

    ba = ba_ref[...]
    lane = lax.broadcasted_iota(jnp.int32, ba.shape, 1)
    b_col = jnp.sum(jnp.where(lane == h, ba, 0.0), axis=1, keepdims=True)
    a_col = jnp.sum(jnp.where(lane == h + GDN_HEADS, ba, 0.0), axis=1, keepdims=True)
    beta = _sigmoid(b_col)
    g = -jnp.exp(alog_ref[h]) * _softplus(a_col + dtb_ref[h])

    ii = lax.broadcasted_iota(jnp.int32, (c, c), 0)
    jj = lax.broadcasted_iota(jnp.int32, (c, c), 1)
    g_row = jnp.sum(jnp.where(ii <= jj, g, 0.0), axis=0, keepdims=True)
    g_col = jnp.sum(jnp.where(ii == jj, g_row, 0.0), axis=1, keepdims=True)
    causal = ii >= jj
    gamma = jnp.where(causal, jnp.exp(jnp.where(causal, g_col - g_row, 0.0)), 0.0)

    kb = k * beta
    a_mat = jnp.where(ii > jj, _dot_nt(kb, k) * gamma, 0.0)
    qk = _dot_nt(q, k) * gamma
    t_inv = _unit_lower_inverse(a_mat, c)

    eg = jnp.exp(g_col)
    uw = _dot(t_inv, jnp.concatenate([v * beta, kb * eg], axis=1))
    u, w = uw[:, :GDN_DK], uw[:, GDN_DK:]
    s = s_ref[...]
    ws_qs = _dot(jnp.concatenate([w, q * eg], axis=0), s)
    v_new = u - ws_qs[:c]
    o = ws_qs[c:] + _dot(qk, v_new)
    g_last = g_col[c - 1:c, :]
    s_ref[...] = s * jnp.exp(g_last) + _dot_tn(k * jnp.exp(g_last - g_col), v_new)

    o = o * lax.rsqrt(jnp.mean(o * o, axis=-1, keepdims=True) + 1e-6) * nw_ref[...]
    o_ref[...] = (o * _silu(z_ref[...].astype(F32))).astype(o_ref.dtype)


def gdn_core(qkvz, ba, conv_w, a_log, dt_bias, norm_w, batch, seq):
    m = qkvz.shape[0]
    c = _pick(seq, GDN_CHUNK)
    nt = seq // c
    nh = GDN_HEADS

    def col(off):
        return pl.BlockSpec((c, GDN_DK), lambda b, h, t: (b * nt + t, off * nh + h))

    def cw(off):
        return pl.BlockSpec((CONV_K, GDN_DK), lambda b, h, t: (0, off * nh + h))

    smem = pl.BlockSpec(memory_space=pltpu.SMEM)
    return pl.pallas_call(
        _gdn_kernel,
        out_shape=jax.ShapeDtypeStruct((m, D_MODEL), BF16),
        grid=(batch, nh, nt),
        in_specs=[smem, smem, col(0), col(1), col(2), col(3),
                  pl.BlockSpec((c, 2 * nh), lambda b, h, t: (b * nt + t, 0)),
                  cw(0), cw(1), cw(2),
                  pl.BlockSpec((1, GDN_DK), lambda b, h, t: (0, 0))],
        out_specs=pl.BlockSpec((c, GDN_DK), lambda b, h, t: (b * nt + t, h)),
        scratch_shapes=[pltpu.VMEM((GDN_DK, GDN_DK), F32),
                        pltpu.VMEM((3, SUBLANES, GDN_DK), F32)],
        compiler_params=_params(("parallel", "parallel", "arbitrary")),
        name="gdn_core",
    )(a_log.astype(F32), dt_bias.astype(F32), qkvz, qkvz, qkvz, qkvz, ba,
      conv_w, conv_w, conv_w, norm_w.reshape(1, GDN_DK))


def gdn_layer(hb, w_in, conv_w, a_log, dt_bias, norm_w, w_out, batch, seq):
    d4 = 4 * D_MODEL
    w_in_b = w_in.astype(BF16)
    qkvz = matmul(hb, w_in_b[:, :d4], BF16)
    ba = matmul(hb, w_in_b[:, d4:], F32)
    o = gdn_core(qkvz, ba, conv_w, a_log, dt_bias, norm_w, batch, seq)
    return matmul(o, w_out.astype(BF16), F32)


def _rwkv_mix_kernel(x_ref, halo_ref, mu_ref, o_ref, *, seq):
    i = pl.program_id(0)
    tm = x_ref.shape[0]
    x = x_ref[...]
    prev = halo_ref[SUBLANES - 1:SUBLANES, :]
    prev = jnp.where((i * tm) % seq == 0, jnp.zeros_like(prev), prev)
    row = lax.broadcasted_iota(jnp.int32, x.shape, 0)
    xprev = jnp.where(row == 0, prev, pltpu.roll(x, 1, axis=0))
    xx = xprev - x
    for p in range(6):
        o_ref[p] = (x + xx * mu_ref[p:p + 1, :]).astype(o_ref.dtype)


def rwkv_mix(h, mu, seq, tm=256):
    m, d = h.shape
    tm = _pick(seq, tm)
    r8 = tm // SUBLANES
    return pl.pallas_call(
        functools.partial(_rwkv_mix_kernel, seq=seq),
        out_shape=jax.ShapeDtypeStruct((6, m, d), BF16),
        grid=(m // tm,),
        in_specs=[pl.BlockSpec((tm, d), lambda i: (i, 0)),
                  pl.BlockSpec((SUBLANES, d), lambda i: (jnp.maximum(i * r8 - 1, 0), 0)),
                  pl.BlockSpec((6, d), lambda i: (0, 0))],
        out_specs=pl.BlockSpec((6, tm, d), lambda i: (0, i, 0)),
        compiler_params=_params(("parallel",)),
        name="rwkv_mix",
    )(h, h, mu)


def _lora_kernel(x_ref, a_ref, b_ref, bias_ref, o_ref, *, act, epi):
    t = jnp.dot(x_ref[...], a_ref[...], preferred_element_type=F32)
    if act == "tanh":
        t = jnp.tanh(t)
    elif act == "sigmoid":
        t = _sigmoid(t)
    y = jnp.dot(t.astype(BF16), b_ref[...], preferred_element_type=F32) + bias_ref[...]
    if epi == "logdecay":
        y = -jnp.exp(-_softplus(-y) - 0.5)
    elif epi == "sigmoid":
        y = _sigmoid(y)
    o_ref[...] = y.astype(o_ref.dtype)


def lora(x, a, b, bias, act, epi, out_dtype, tm=512):
    m, d = x.shape
    r = a.shape[1]
    n = b.shape[1]
    tm = _pick(m, tm)
    return pl.pallas_call(
        functools.partial(_lora_kernel, act=act, epi=epi),
        out_shape=jax.ShapeDtypeStruct((m, n), out_dtype),
        grid=(m // tm,),
        in_specs=[pl.BlockSpec((tm, d), lambda i: (i, 0)),
                  pl.BlockSpec((d, r), lambda i: (0, 0)),
                  pl.BlockSpec((r, n), lambda i: (0, 0)),
                  pl.BlockSpec((1, n), lambda i: (0, 0))],
        out_specs=pl.BlockSpec((tm, n), lambda i: (i, 0)),
        compiler_params=_params(("parallel",)),
        name="lora_" + epi,
    )(x, a.astype(BF16), b.astype(BF16), bias.reshape(1, n).astype(F32))


def _group_sum(x, lo):
    s0 = jnp.sum(jnp.where(lo, x, 0.0), axis=1, keepdims=True)
    s1 = jnp.sum(jnp.where(lo, 0.0, x), axis=1, keepdims=True)
    return jnp.where(lo, s0, s1)


def _rwkv_kernel(r_ref, k_ref, v_ref, lw_ref, a_ref, g_ref, kk_ref, ka_ref, rk_ref,
                 gng_ref, gnb_ref, o_ref, s_ref):
    t = pl.program_id(2)
    c = r_ref.shape[0]

    @pl.when(t == 0)
    def _():
        s_ref[...] = jnp.zeros_like(s_ref)

    r = r_ref[...].astype(F32)
    k = k_ref[...].astype(F32)
    v = v_ref[...].astype(F32)
    lw = lw_ref[...]
    a = a_ref[...]
    lane = lax.broadcasted_iota(jnp.int32, (c, LANES), 1)
    row = lax.broadcasted_iota(jnp.int32, (c, LANES), 0)
    lo = lane < RWKV_HEAD

    kk = k * kk_ref[...]
    kk = kk * lax.rsqrt(_group_sum(kk * kk, lo) + 1e-6)
    k2 = k * (1.0 + (a - 1.0) * ka_ref[...])
    kka = kk * a

    cl = lw
    sft = 1
    while sft < c:
        cl = cl + jnp.where(row >= sft, pltpu.roll(cl, sft, axis=0), 0.0)
        sft *= 2
    e_pos = jnp.exp(cl)
    e_neg = jnp.exp(-cl)
    e_prev = jnp.exp(cl - lw)

    def stack(x):
        return jnp.concatenate([jnp.where(lo, x, 0.0), jnp.where(lo, 0.0, x)], axis=0)

    kr = jnp.concatenate([stack(e_prev * kk), stack(e_pos * r)], axis=0)
    ak = jnp.concatenate([stack(e_neg * kka), stack(e_neg * k2)], axis=0)
    vs = stack(v)
    c2 = 2 * c
    scores = _dot_nt(kr, ak)
    ti = lax.broadcasted_iota(jnp.int32, (c2, c2), 0) % c
    tj = lax.broadcasted_iota(jnp.int32, (c2, c2), 1) % c
    strict = ti > tj
    incl = ti >= tj
    a_aa = jnp.where(strict, scores[:c2, :c2], 0.0)
    a_ak = jnp.where(strict, scores[:c2, c2:], 0.0)
    b_ra = jnp.where(incl, scores[c2:, :c2], 0.0)
    b_rk = jnp.where(incl, scores[c2:, c2:], 0.0)
    t_inv = _unit_lower_inverse(a_aa, c)

    s = s_ref[...]
    kz_rz = _dot_nt(kr, s)
    u = _dot(t_inv, -kz_rz[:c2] - _dot(a_ak, vs))
    uv = jnp.concatenate([u, vs], axis=0)
    ys = kz_rz[c2:] + _dot(jnp.concatenate([b_ra, b_rk], axis=1), uv)
    y = ys[:c] + ys[c:]

    cl_last = cl[c - 1:c, :]
    dec = jnp.exp(cl_last - cl)
    ak2 = jnp.concatenate([stack(dec * kka), stack(dec * k2)], axis=0)
    s_ref[...] = s * jnp.exp(cl_last) + _dot_tn(uv, ak2)

    inv_n = 1.0 / RWKV_HEAD
    mean = _group_sum(y, lo) * inv_n
    yc = y - mean
    var = _group_sum(yc * yc, lo) * inv_n
    yn = yc * lax.rsqrt(var + RWKV_GN_EPS) * gng_ref[...] + gnb_ref[...]
    bonus = _group_sum(r * k2 * rk_ref[...], lo) * v
    o_ref[...] = ((yn + bonus) * g_ref[...].astype(F32)).astype(o_ref.dtype)


def rwkv_core(r, k, v, lw, a, g, k_k, k_a, r_k, gn_g, gn_b, batch, seq):
    m, d = r.shape
    c = _pick(seq, RWKV_CHUNK)
    nt = seq // c
    npair = d // LANES
    tok = pl.BlockSpec((c, LANES), lambda b, p, t: (b * nt + t, p))
    vec = pl.BlockSpec((1, LANES), lambda b, p, t: (0, p))
    row = lambda x: x.reshape(1, d).astype(F32)
    return pl.pallas_call(
        _rwkv_kernel,
        out_shape=jax.ShapeDtypeStruct((m, d), BF16),
        grid=(batch, npair, nt),
        in_specs=[tok] * 6 + [vec] * 5,
        out_specs=tok,
        scratch_shapes=[pltpu.VMEM((LANES, LANES), F32)],
        compiler_params=_params(("parallel", "parallel", "arbitrary")),
        name="rwkv_core",
    )(r, k, v, lw, a, g, row(k_k), row(k_a), row(r_k), row(gn_g), row(gn_b))


def rwkv_layer(h, mu, w_rkv, w0, w1, w2, a0, a1, a2, g1, g2, k_k, k_a, r_k, gn_g, gn_b,
               w_out, batch, seq):
    xm = rwkv_mix(h, mu, seq)
    w_rkv_b = w_rkv.astype(BF16)
    r = matmul(xm[0], w_rkv_b[0], BF16)
    k = matmul(xm[1], w_rkv_b[1], BF16)
    v = matmul(xm[2], w_rkv_b[2], BF16)
    lw = lora(xm[3], w1, w2, w0, "tanh", "logdecay", F32)
    a = lora(xm[4], a1, a2, a0, "none", "sigmoid", F32)
    g = lora(xm[5], g1, g2, jnp.zeros_like(a0), "sigmoid", "none", BF16)
    y = rwkv_core(r, k, v, lw, a, g, k_k, k_a, r_k, gn_g, gn_b, batch, seq)
    return matmul(y, w_out.astype(BF16), F32)


def _lru_kernel(gate_ref, rec_ref, cw_ref, cb_ref, wgx_ref, bgx_ref, wga_ref, bga_ref,
                lam_ref, o_ref, tail_ref, h_ref):
    t = pl.program_id(2)
    c = rec_ref.shape[0]

    @pl.when(t == 0)
    def _():
        tail_ref[...] = jnp.zeros_like(tail_ref)
        h_ref[...] = jnp.zeros_like(h_ref)

    x = rec_ref[...].astype(F32)
    u = _causal_conv(x, tail_ref[...], cw_ref[...]) + cb_ref[...]
    tail_ref[...] = x[c - SUBLANES:]
    ub = u.astype(BF16)
    i_t = _sigmoid(jnp.dot(ub, wgx_ref[...], preferred_element_type=F32) + bgx_ref[...])
    r_t = _sigmoid(jnp.dot(ub, wga_ref[...], preferred_element_type=F32) + bga_ref[...])
    log_a = -LRU_C * r_t * _softplus(-lam_ref[...])
    a = jnp.exp(log_a)
    th = jnp.tanh(log_a)
    b = jnp.sqrt(-2.0 * th / (1.0 - th)) * (i_t * u)

    row = lax.broadcasted_iota(jnp.int32, a.shape, 0)
    sft = 1
    while sft < c:
        keep = row >= sft
        a_s = jnp.where(keep, pltpu.roll(a, sft, axis=0), 1.0)
        b_s = jnp.where(keep, pltpu.roll(b, sft, axis=0), 0.0)
        b = a * b_s + b
        a = a * a_s
        sft *= 2
    hcur = b + a * h_ref[0:1, :]
    h_ref[0:1, :] = hcur[c - 1:c, :]

    gt = gate_ref[...].astype(F32)
    gelu = 0.5 * gt * (1.0 + jnp.tanh(math.sqrt(2.0 / math.pi) * (gt + 0.044715 * gt * gt * gt)))
    o_ref[...] = (hcur * gelu).astype(o_ref.dtype)


def lru_core(proj, conv_w, conv_b, w_gx, b_gx, w_ga, b_ga, lam, batch, seq):
    m = proj.shape[0]
    c = _pick(seq, LRU_TBLOCK)
    nt = seq // c
    nb = LRU_BLOCKS
    w = LRU_BLOCK
    vec = pl.BlockSpec((1, w), lambda b, j, t: (0, j))
    blk = pl.BlockSpec((None, w, w), lambda b, j, t: (j, 0, 0))
    row = lambda x: x.reshape(1, nb * w).astype(F32)
    return pl.pallas_call(
        _lru_kernel,
        out_shape=jax.ShapeDtypeStruct((m, nb * w), BF16),
        grid=(batch, nb, nt),
        in_specs=[pl.BlockSpec((c, w), lambda b, j, t: (b * nt + t, j)),
                  pl.BlockSpec((c, w), lambda b, j, t: (b * nt + t, nb + j)),
                  pl.BlockSpec((CONV_K, w), lambda b, j, t: (0, j)),
                  vec, blk, vec, blk, vec, vec],
        out_specs=pl.BlockSpec((c, w), lambda b, j, t: (b * nt + t, j)),
        scratch_shapes=[pltpu.VMEM((SUBLANES, w), F32), pltpu.VMEM((SUBLANES, w), F32)],
        compiler_params=_params(("parallel", "parallel", "arbitrary")),
        name="lru_core",
    )(proj, proj, conv_w, row(conv_b), w_gx.astype(BF16), row(b_gx), w_ga.astype(BF16),
      row(b_ga), row(lam))


def lru_layer(hb, w_in, conv_w, conv_b, w_gx, b_gx, w_ga, b_ga, lam, w_out, batch, seq):
    proj = matmul(hb, w_in.astype(BF16), BF16)
    y = lru_core(proj, conv_w, conv_b, w_gx, b_gx, w_ga, b_ga, lam, batch, seq)
    return matmul(y, w_out.astype(BF16), F32)


def dense_ffn(hb, w_gu, w_down):
    act = swiglu_up(hb, w_gu.astype(BF16))
    return matmul(act, w_down.astype(BF16), F32)


def moe_ffn(h, hb, router_w, router_b, w_gu, w_down):
    comb = router(h, router_w, router_b)
    act = moe_up(hb, w_gu.astype(BF16), comb)
    ne, fe, d = w_down.shape
    return matmul(act, w_down.astype(BF16).reshape(ne * fe, d), F32)


def kernel(x, ln_g, ln_b, gdn_w_in, gdn_conv_w, gdn_a_log, gdn_dt_bias, gdn_norm_w, gdn_w_out, rwkv_mu, rwkv_w_rkv, rwkv_w0, rwkv_w1, rwkv_w2, rwkv_a0, rwkv_a1, rwkv_a2, rwkv_g1, rwkv_g2, rwkv_k_k, rwkv_k_a, rwkv_r_k, rwkv_gn_g, rwkv_gn_b, rwkv_w_out, lru_w_in, lru_conv_w, lru_conv_b, lru_w_gx, lru_b_gx, lru_w_ga, lru_b_ga, lru_lam, lru_w_out, ffn_w_gu, ffn_w_down, moe_router_w, moe_router_b, moe_w_gu, moe_w_down):
    batch, seq, d = x.shape
    h = x.reshape(batch * seq, d)
    hb = h.astype(BF16)
    for i in range(DEPTH):
        m, s = i % 3, i // 3
        if m == 0:
            y = gdn_layer(hb, gdn_w_in[s], gdn_conv_w[s], gdn_a_log[s], gdn_dt_bias[s],
                          gdn_norm_w[s], gdn_w_out[s], batch, seq)
        elif m == 1:
            y = rwkv_layer(h, rwkv_mu[s], rwkv_w_rkv[s], rwkv_w0[s], rwkv_w1[s], rwkv_w2[s],
                           rwkv_a0[s], rwkv_a1[s], rwkv_a2[s], rwkv_g1[s], rwkv_g2[s],
                           rwkv_k_k[s], rwkv_k_a[s], rwkv_r_k[s], rwkv_gn_g[s], rwkv_gn_b[s],
                           rwkv_w_out[s], batch, seq)
        else:
            y = lru_layer(hb, lru_w_in[s], lru_conv_w[s], lru_conv_b[s], lru_w_gx[s],
                          lru_b_gx[s], lru_w_ga[s], lru_b_ga[s], lru_lam[s], lru_w_out[s],
                          batch, seq)
        h, hb = ln_residual(h, y, ln_g[i, 0], ln_b[i, 0])
        f = i // 2
        if i % 2 == 0:
            y = dense_ffn(hb, ffn_w_gu[f], ffn_w_down[f])
        else:
            y = moe_ffn(h, hb, moe_router_w[f], moe_router_b[f], moe_w_gu[f], moe_w_down[f])
        h, hb = ln_residual(h, y, ln_g[i, 1], ln_b[i, 1])
    return h.reshape(batch, seq, d)
```

```python
import functools
import math

import jax
import jax.numpy as jnp
import numpy as np
from jax import lax
from jax.experimental import pallas as pl
from jax.experimental.pallas import tpu as pltpu

F32 = jnp.float32
BF16 = jnp.bfloat16

D_MODEL = 4096
DEPTH = 4
DN_ALPHA = (2 * DEPTH) ** 0.25
LN_EPS = 1e-5
CONV_K = 4
GDN_DK = 128
GDN_HEADS = D_MODEL // GDN_DK
RWKV_HEAD = 64
RWKV_GN_EPS = 64e-5
LRU_BLOCK = 256
LRU_BLOCKS = D_MODEL // LRU_BLOCK
LRU_C = 8.0
N_EXPERTS = 8

V7X_VMEM_BYTES = 64 * 1024 * 1024
VMEM_LIMIT = V7X_VMEM_BYTES - 8 * 1024 * 1024
LANES = 128
SUBLANES = 8

GDN_CHUNK = 256
RWKV_CHUNK = 64
LRU_TBLOCK = 256
INV_BASE = 16
GDN_HEADS_PER_STEP = 4
RWKV_PAIRS_PER_STEP = 16


def _params(sem):
    return pltpu.CompilerParams(dimension_semantics=sem, vmem_limit_bytes=VMEM_LIMIT)


def _dot(a, b):
    return jnp.dot(a.astype(BF16), b.astype(BF16), preferred_element_type=F32)


def _dot_nt(a, b):
    return lax.dot_general(a.astype(BF16), b.astype(BF16), (((1,), (1,)), ((), ())),
                           preferred_element_type=F32)


def _dot_tn(a, b):
    return lax.dot_general(a.astype(BF16), b.astype(BF16), (((0,), (0,)), ((), ())),
                           preferred_element_type=F32)


def _sigmoid(x):
    return 1.0 / (1.0 + jnp.exp(-x))


def _silu(x):
    return x * _sigmoid(x)


def _softplus(x):
    return jnp.maximum(x, 0.0) + jnp.log1p(jnp.exp(-jnp.abs(x)))


def _pick(n, pref):
    t = min(n, pref)
    while n % t:
        t //= 2
    return t


def _mm_kernel(x_ref, w_ref, o_ref, *, nk):
    acc = jnp.dot(x_ref[...], w_ref[...], preferred_element_type=F32)
    if nk == 1:
        o_ref[...] = acc.astype(o_ref.dtype)
    else:
        k = pl.program_id(2)

        @pl.when(k == 0)
        def _():
            o_ref[...] = acc

        @pl.when(k > 0)
        def _():
            o_ref[...] += acc


def matmul(x, w, out_dtype, bm=1024, bn=1024, bk=4096):
    m, kdim = x.shape
    n = w.shape[1]
    bm, bn, bk = _pick(m, bm), _pick(n, bn), _pick(kdim, bk)
    nk = kdim // bk
    if nk > 1:
        assert out_dtype == F32
    return pl.pallas_call(
        functools.partial(_mm_kernel, nk=nk),
        out_shape=jax.ShapeDtypeStruct((m, n), out_dtype),
        grid=(m // bm, n // bn, nk),
        in_specs=[pl.BlockSpec((bm, bk), lambda i, j, k: (i, k)),
                  pl.BlockSpec((bk, bn), lambda i, j, k: (k, j))],
        out_specs=pl.BlockSpec((bm, bn), lambda i, j, k: (i, j)),
        compiler_params=_params(("parallel", "parallel", "arbitrary")),
        name="matmul",
    )(x, w)


def _swiglu_kernel(x_ref, wg_ref, wu_ref, o_ref):
    x = x_ref[...]
    g = jnp.dot(x, wg_ref[...], preferred_element_type=F32)
    u = jnp.dot(x, wu_ref[...], preferred_element_type=F32)
    o_ref[...] = (_silu(g) * u).astype(o_ref.dtype)


def swiglu_up(x, w_gu, bm=1024, bn=512):
    m, d = x.shape
    f = w_gu.shape[1] // 2
    bm, bn = _pick(m, bm), _pick(f, bn)
    nj = f // bn
    return pl.pallas_call(
        _swiglu_kernel,
        out_shape=jax.ShapeDtypeStruct((m, f), BF16),
        grid=(m // bm, nj),
        in_specs=[pl.BlockSpec((bm, d), lambda i, j: (i, 0)),
                  pl.BlockSpec((d, bn), lambda i, j: (0, j)),
                  pl.BlockSpec((d, bn), lambda i, j: (0, j + nj))],
        out_specs=pl.BlockSpec((bm, bn), lambda i, j: (i, j)),
        compiler_params=_params(("parallel", "parallel")),
        name="swiglu_up",
    )(x, w_gu, w_gu)


def _moe_up_kernel(x_ref, wg_ref, wu_ref, comb_ref, o_ref):
    e = pl.program_id(1)
    x = x_ref[...]
    g = jnp.dot(x, wg_ref[...], preferred_element_type=F32)
    u = jnp.dot(x, wu_ref[...], preferred_element_type=F32)
    comb = comb_ref[...]
    lane = lax.broadcasted_iota(jnp.int32, comb.shape, 1)
    ce = jnp.sum(jnp.where(lane == e, comb, 0.0), axis=1, keepdims=True)
    o_ref[...] = (_silu(g) * u * ce).astype(o_ref.dtype)


def moe_up(x, w_gu, comb, bm=1024, bn=512):
    m, d = x.shape
    ne = w_gu.shape[0]
    f = w_gu.shape[2] // 2
    bm, bn = _pick(m, bm), _pick(f, bn)
    nj = f // bn
    return pl.pallas_call(
        _moe_up_kernel,
        out_shape=jax.ShapeDtypeStruct((m, ne * f), BF16),
        grid=(m // bm, ne, nj),
        in_specs=[pl.BlockSpec((bm, d), lambda i, e, j: (i, 0)),
                  pl.BlockSpec((None, d, bn), lambda i, e, j: (e, 0, j)),
                  pl.BlockSpec((None, d, bn), lambda i, e, j: (e, 0, j + nj)),
                  pl.BlockSpec((bm, LANES), lambda i, e, j: (i, 0))],
        out_specs=pl.BlockSpec((bm, bn), lambda i, e, j: (i, e * nj + j)),
        compiler_params=_params(("parallel", "parallel", "parallel")),
        name="moe_up",
    )(x, w_gu, w_gu, comb)


def _router_kernel(x_ref, w_ref, b_ref, o_ref):
    logits = jnp.dot(x_ref[...], w_ref[...], preferred_element_type=F32,
                     precision=lax.Precision.HIGHEST) + b_ref[...]
    lane = lax.broadcasted_iota(jnp.int32, logits.shape, 1)
    neg = jnp.float32(-jnp.inf)
    logits = jnp.where(lane < N_EXPERTS, logits, neg)
    m1 = jnp.max(logits, axis=1, keepdims=True)
    i1 = jnp.min(jnp.where(logits == m1, lane, LANES), axis=1, keepdims=True)
    sel1 = lane == i1
    rest = jnp.where(sel1, neg, logits)
    m2 = jnp.max(rest, axis=1, keepdims=True)
    i2 = jnp.min(jnp.where(rest == m2, lane, LANES), axis=1, keepdims=True)
    sel2 = lane == i2
    e2 = jnp.exp(m2 - m1)
    p1 = 1.0 / (1.0 + e2)
    p2 = e2 / (1.0 + e2)
    o_ref[...] = jnp.where(sel1, p1, 0.0) + jnp.where(sel2, p2, 0.0)


def router(x, rw, rb, bm=512):
    m, d = x.shape
    bm = _pick(m, bm)
    rw_p = jnp.zeros((d, LANES), F32).at[:, :N_EXPERTS].set(rw.astype(F32))
    rb_p = jnp.zeros((1, LANES), F32).at[0, :N_EXPERTS].set(rb.astype(F32))
    return pl.pallas_call(
        _router_kernel,
        out_shape=jax.ShapeDtypeStruct((m, LANES), F32),
        grid=(m // bm,),
        in_specs=[pl.BlockSpec((bm, d), lambda i: (i, 0)),
                  pl.BlockSpec((d, LANES), lambda i: (0, 0)),
                  pl.BlockSpec((1, LANES), lambda i: (0, 0))],
        out_specs=pl.BlockSpec((bm, LANES), lambda i: (i, 0)),
        compiler_params=_params(("parallel",)),
        name="router",
    )(x, rw_p, rb_p)


def _ln_kernel(h_ref, y_ref, g_ref, b_ref, o_ref, ob_ref):
    x = DN_ALPHA * h_ref[...] + y_ref[...]
    mu = jnp.mean(x, axis=-1, keepdims=True)
    xc = x - mu
    var = jnp.mean(xc * xc, axis=-1, keepdims=True)
    o = xc * lax.rsqrt(var + LN_EPS) * g_ref[...] + b_ref[...]
    o_ref[...] = o
    ob_ref[...] = o.astype(BF16)


def ln_residual(h, y, g, b, tm=256):
    m, d = h.shape
    tm = _pick(m, tm)
    row = pl.BlockSpec((tm, d), lambda i: (i, 0))
    vec = pl.BlockSpec((1, d), lambda i: (0, 0))
    return pl.pallas_call(
        _ln_kernel,
        out_shape=(jax.ShapeDtypeStruct((m, d), F32), jax.ShapeDtypeStruct((m, d), BF16)),
        grid=(m // tm,),
        in_specs=[row, row, vec, vec],
        out_specs=(row, row),
        compiler_params=_params(("parallel",)),
        name="ln_residual",
    )(h, y, g.reshape(1, d), b.reshape(1, d))


def _causal_conv(x, tail, w):
    c = x.shape[0]
    xs = jnp.concatenate([tail, x], axis=0)
    y = xs[SUBLANES:] * w[CONV_K - 1:CONV_K]
    for j in range(CONV_K - 1):
        y = y + pltpu.roll(xs, CONV_K - 1 - j, axis=0)[SUBLANES:SUBLANES + c] * w[j:j + 1]
    return y


def _tri_masks(n, period, top):
    i = np.arange(n)[:, None]
    j = np.arange(n)[None, :]
    ms = [i == j, (i // INV_BASE) == (j // INV_BASE)]
    size = INV_BASE
    while size < top:
        bi, bj = i // size, j // size
        ms.append((bi // 2 == bj // 2) & (bi % 2 == 1) & (bj % 2 == 0))
        size *= 2
    ti, tj = i % period, j % period
    ms += [ti >= tj, ti > tj, ti <= tj]
    return jnp.asarray(np.stack(ms).astype(np.float32))


def _n_merge_levels(top):
    return (top // INV_BASE).bit_length() - 1


def _unit_lower_inverse(a_strict, m_ref, top):
    ps = [-a * m_ref[1] for a in a_strict]
    xs = [m_ref[0] + p for p in ps]
    k = 2
    while k < INV_BASE:
        ps = [_dot(p, p) for p in ps]
        xs = [x + _dot(x, p) for x, p in zip(xs, ps)]
        k *= 2
    for lvl in range(_n_merge_levels(top)):
        ys = [_dot(a * m_ref[2 + lvl], x) for a, x in zip(a_strict, xs)]
        xs = [x - _dot(x, y) for x, y in zip(xs, ys)]
    return xs


def _gdn_kernel(alog_ref, dtb_ref, q_ref, k_ref, v_ref, z_ref, ba_ref,
                cwq_ref, cwk_ref, cwv_ref, nw_ref, m_ref, o_ref, s_ref, tail_ref, *, heads):
    hg = pl.program_id(1)
    t = pl.program_id(2)
    c = q_ref.shape[0]
    nl = _n_merge_levels(c)
    causal, strict, upper = m_ref[2 + nl], m_ref[3 + nl], m_ref[4 + nl]
    eye = m_ref[0]

    @pl.when(t == 0)
    def _():
        s_ref[...] = jnp.zeros_like(s_ref)
        tail_ref[...] = jnp.zeros_like(tail_ref)

    ba = ba_ref[...]
    lane = lax.broadcasted_iota(jnp.int32, ba.shape, 1)

    hs = range(heads)
    sls = [slice(gi * GDN_DK, (gi + 1) * GDN_DK) for gi in hs]

    def conv_silu(x_ref, cw_ref, slot, sl):
        x = x_ref[:, sl].astype(F32)
        y = _causal_conv(x, tail_ref[slot, :, sl], cw_ref[:, sl])
        tail_ref[slot, :, sl] = x[c - SUBLANES:]
        return _silu(y)

    qs = [conv_silu(q_ref, cwq_ref, 0, sl) for sl in sls]
    ks = [conv_silu(k_ref, cwk_ref, 1, sl) for sl in sls]
    vs = [conv_silu(v_ref, cwv_ref, 2, sl) for sl in sls]
    qs = [q * lax.rsqrt(jnp.sum(q * q, axis=-1, keepdims=True) + 1e-6) * (GDN_DK ** -0.5)
          for q in qs]
    ks = [k * lax.rsqrt(jnp.sum(k * k, axis=-1, keepdims=True) + 1e-6) for k in ks]

    betas, g_cols, gammas = [], [], []
    for gi in hs:
        h = hg * heads + gi
        b_col = jnp.sum(jnp.where(lane == h, ba, 0.0), axis=1, keepdims=True)
        a_col = jnp.sum(jnp.where(lane == h + GDN_HEADS, ba, 0.0), axis=1, keepdims=True)
        betas.append(_sigmoid(b_col))
        g = -jnp.exp(alog_ref[h]) * _softplus(a_col + dtb_ref[h])
        g_row = jnp.sum(g * upper, axis=0, keepdims=True)
        g_col = jnp.sum(g_row * eye, axis=1, keepdims=True)
        g_cols.append(g_col)
        gammas.append(jnp.exp((g_col - g_row) * causal) * causal)

    kbs = [k * b for k, b in zip(ks, betas)]
    a_mats = [_dot_nt(kb, k) * (gm * strict) for kb, k, gm in zip(kbs, ks, gammas)]
    qks = [_dot_nt(q, k) * gm for q, k, gm in zip(qs, ks, gammas)]
    t_invs = _unit_lower_inverse(a_mats, m_ref, c)

    egs = [jnp.exp(gc) for gc in g_cols]
    uws = [_dot(ti, jnp.concatenate([v * b, kb * eg], axis=1))
           for ti, v, b, kb, eg in zip(t_invs, vs, betas, kbs, egs)]
    ss = [s_ref[gi] for gi in hs]
    ws_qs = [_dot(jnp.concatenate([uw[:, GDN_DK:], q * eg], axis=0), s)
             for uw, q, eg, s in zip(uws, qs, egs, ss)]
    v_news = [uw[:, :GDN_DK] - wq[:c] for uw, wq in zip(uws, ws_qs)]
    os_ = [wq[c:] + _dot(qk, vn) for wq, qk, vn in zip(ws_qs, qks, v_news)]
    for gi in hs:
        g_last = g_cols[gi][c - 1:c, :]
        s_ref[gi] = (ss[gi] * jnp.exp(g_last)
                     + _dot_tn(ks[gi] * jnp.exp(g_last - g_cols[gi]), v_news[gi]))
    for gi in hs:
        o = os_[gi]
        o = o * lax.rsqrt(jnp.mean(o * o, axis=-1, keepdims=True) + 1e-6) * nw_ref[...]
        o_ref[:, sls[gi]] = (o * _silu(z_ref[:, sls[gi]].astype(F32))).astype(o_ref.dtype)


def gdn_core(qkvz, ba, conv_w, a_log, dt_bias, norm_w, batch, seq):
    m = qkvz.shape[0]
    c = _pick(seq, GDN_CHUNK)
    nt = seq // c
    nh = GDN_HEADS
    g = GDN_HEADS_PER_STEP
    ng = nh // g
    w = g * GDN_DK
    masks = _tri_masks(c, c, c)

    def col(off):
        return pl.BlockSpec((c, w), lambda b, h, t: (b * nt + t, off * ng + h))

    def cw(off):
        return pl.BlockSpec((CONV_K, w), lambda b, h, t: (0, off * ng + h))

    smem = pl.BlockSpec(memory_space=pltpu.SMEM)
    return pl.pallas_call(
        functools.partial(_gdn_kernel, heads=g),
        out_shape=jax.ShapeDtypeStruct((m, D_MODEL), BF16),
        grid=(batch, ng, nt),
        in_specs=[smem, smem, col(0), col(1), col(2), col(3),
                  pl.BlockSpec((c, 2 * nh), lambda b, h, t: (b * nt + t, 0)),
                  cw(0), cw(1), cw(2),
                  pl.BlockSpec((1, GDN_DK), lambda b, h, t: (0, 0)),
                  pl.BlockSpec(masks.shape, lambda b, h, t: (0, 0, 0))],
        out_specs=pl.BlockSpec((c, w), lambda b, h, t: (b * nt + t, h)),
        scratch_shapes=[pltpu.VMEM((g, GDN_DK, GDN_DK), F32),
                        pltpu.VMEM((3, SUBLANES, w), F32)],
        compiler_params=_params(("parallel", "parallel", "arbitrary")),
        name="gdn_core",
    )(a_log.astype(F32), dt_bias.astype(F32), qkvz, qkvz, qkvz, qkvz, ba,
      conv_w, conv_w, conv_w, norm_w.reshape(1, GDN_DK), masks)


def gdn_layer(hb, w_in, conv_w, a_log, dt_bias, norm_w, w_out, batch, seq):
    d4 = 4 * D_MODEL
    w_in_b = w_in.astype(BF16)
    qkvz = matmul(hb, w_in_b[:, :d4], BF16)
    ba = matmul(hb, w_in_b[:, d4:], F32)
    o = gdn_core(qkvz, ba, conv_w, a_log, dt_bias, norm_w, batch, seq)
    return matmul(o, w_out.astype(BF16), F32)


def _rwkv_mix_kernel(x_ref, halo_ref, mu_ref, o_ref, *, seq):
    i = pl.program_id(0)
    tm = x_ref.shape[0]
    x = x_ref[...]
    prev = halo_ref[SUBLANES - 1:SUBLANES, :]
    prev = jnp.where((i * tm) % seq == 0, jnp.zeros_like(prev), prev)
    row = lax.broadcasted_iota(jnp.int32, x.shape, 0)
    xprev = jnp.where(row == 0, prev, pltpu.roll(x, 1, axis=0))
    xx = xprev - x
    for p in range(6):
        o_ref[p] = (x + xx * mu_ref[p:p + 1, :]).astype(o_ref.dtype)


def rwkv_mix(h, mu, seq, tm=256):
    m, d = h.shape
    tm = _pick(seq, tm)
    r8 = tm // SUBLANES
    return pl.pallas_call(
        functools.partial(_rwkv_mix_kernel, seq=seq),
        out_shape=jax.ShapeDtypeStruct((6, m, d), BF16),
        grid=(m // tm,),
        in_specs=[pl.BlockSpec((tm, d), lambda i: (i, 0)),
                  pl.BlockSpec((SUBLANES, d), lambda i: (jnp.maximum(i * r8 - 1, 0), 0)),
                  pl.BlockSpec((6, d), lambda i: (0, 0))],
        out_specs=pl.BlockSpec((6, tm, d), lambda i: (0, i, 0)),
        compiler_params=_params(("parallel",)),
        name="rwkv_mix",
    )(h, h, mu)


def _lora_kernel(x_ref, a_ref, b_ref, bias_ref, o_ref, *, act, epi):
    t = jnp.dot(x_ref[...], a_ref[...], preferred_element_type=F32)
    if act == "tanh":
        t = jnp.tanh(t)
    elif act == "sigmoid":
        t = _sigmoid(t)
    y = jnp.dot(t.astype(BF16), b_ref[...], preferred_element_type=F32) + bias_ref[...]
    if epi == "logdecay":
        y = -jnp.exp(-_softplus(-y) - 0.5)
    elif epi == "sigmoid":
        y = _sigmoid(y)
    o_ref[...] = y.astype(o_ref.dtype)


def lora(x, a, b, bias, act, epi, out_dtype, tm=512):
    m, d = x.shape
    r = a.shape[1]
    n = b.shape[1]
    tm = _pick(m, tm)
    return pl.pallas_call(
        functools.partial(_lora_kernel, act=act, epi=epi),
        out_shape=jax.ShapeDtypeStruct((m, n), out_dtype),
        grid=(m // tm,),
        in_specs=[pl.BlockSpec((tm, d), lambda i: (i, 0)),
                  pl.BlockSpec((d, r), lambda i: (0, 0)),
                  pl.BlockSpec((r, n), lambda i: (0, 0)),
                  pl.BlockSpec((1, n), lambda i: (0, 0))],
        out_specs=pl.BlockSpec((tm, n), lambda i: (i, 0)),
        compiler_params=_params(("parallel",)),
        name="lora_" + epi,
    )(x, a.astype(BF16), b.astype(BF16), bias.reshape(1, n).astype(F32))


def _group_sum(x, lo):
    s0 = jnp.sum(jnp.where(lo, x, 0.0), axis=1, keepdims=True)
    s1 = jnp.sum(jnp.where(lo, 0.0, x), axis=1, keepdims=True)
    return jnp.where(lo, s0, s1)


def _rwkv_kernel(r_ref, k_ref, v_ref, lw_ref, a_ref, g_ref, kk_ref, ka_ref, rk_ref,
                 gng_ref, gnb_ref, m_ref, o_ref, s_ref, *, pairs):
    t = pl.program_id(2)
    c = r_ref.shape[0]
    c2 = 2 * c
    nl = _n_merge_levels(c)
    incl, strict = m_ref[2 + nl], m_ref[3 + nl]

    @pl.when(t == 0)
    def _():
        s_ref[...] = jnp.zeros_like(s_ref)

    lane = lax.broadcasted_iota(jnp.int32, (c, LANES), 1)
    row = lax.broadcasted_iota(jnp.int32, (c, LANES), 0)
    lo = lane < RWKV_HEAD

    def stack(x):
        return jnp.concatenate([jnp.where(lo, x, 0.0), jnp.where(lo, 0.0, x)], axis=0)

    ps = range(pairs)
    sls = [slice(pi * LANES, (pi + 1) * LANES) for pi in ps]
    rs = [r_ref[:, sl].astype(F32) for sl in sls]
    ks = [k_ref[:, sl].astype(F32) for sl in sls]
    vs = [v_ref[:, sl].astype(F32) for sl in sls]
    lws = [lw_ref[:, sl] for sl in sls]
    a_s = [a_ref[:, sl] for sl in sls]

    kks = [k * kk_ref[:, sl] for k, sl in zip(ks, sls)]
    kks = [kk * lax.rsqrt(_group_sum(kk * kk, lo) + 1e-6) for kk in kks]
    k2s = [k * (1.0 + (a - 1.0) * ka_ref[:, sl]) for k, a, sl in zip(ks, a_s, sls)]
    kkas = [kk * a for kk, a in zip(kks, a_s)]

    cls = lws
    sft = 1
    while sft < c:
        cls = [cl + jnp.where(row >= sft, pltpu.roll(cl, sft, axis=0), 0.0) for cl in cls]
        sft *= 2

    krs = [jnp.concatenate([stack(jnp.exp(cl - lw) * kk), stack(jnp.exp(cl) * r)], axis=0)
           for cl, lw, kk, r in zip(cls, lws, kks, rs)]
    aks = [jnp.concatenate([stack(jnp.exp(-cl) * kka), stack(jnp.exp(-cl) * k2)], axis=0)
           for cl, kka, k2 in zip(cls, kkas, k2s)]
    vss = [stack(v) for v in vs]
    scs = [_dot_nt(kr, ak) for kr, ak in zip(krs, aks)]
    t_invs = _unit_lower_inverse([sc[:c2, :c2] * strict for sc in scs], m_ref, c)

    ss = [s_ref[pi] for pi in ps]
    kz_rzs = [_dot_nt(kr, s) for kr, s in zip(krs, ss)]
    avs = [_dot(sc[:c2, c2:] * strict, v) for sc, v in zip(scs, vss)]
    us = [_dot(ti, -kz[:c2] - av) for ti, kz, av in zip(t_invs, kz_rzs, avs)]
    uvs = [jnp.concatenate([u, v], axis=0) for u, v in zip(us, vss)]
    yss = [kz[c2:] + _dot(sc[c2:, :] * jnp.concatenate([incl, incl], axis=1), uv)
           for kz, sc, uv in zip(kz_rzs, scs, uvs)]
    for pi in ps:
        cl_last = cls[pi][c - 1:c, :]
        dec = jnp.exp(cl_last - cls[pi])
        ak2 = jnp.concatenate([stack(dec * kkas[pi]), stack(dec * k2s[pi])], axis=0)
        s_ref[pi] = ss[pi] * jnp.exp(cl_last) + _dot_tn(uvs[pi], ak2)

    inv_n = 1.0 / RWKV_HEAD
    for pi in ps:
        sl = sls[pi]
        y = yss[pi][:c] + yss[pi][c:]
        mean = _group_sum(y, lo) * inv_n
        yc = y - mean
        var = _group_sum(yc * yc, lo) * inv_n
        yn = yc * lax.rsqrt(var + RWKV_GN_EPS) * gng_ref[:, sl] + gnb_ref[:, sl]
        bonus = _group_sum(rs[pi] * k2s[pi] * rk_ref[:, sl], lo) * vs[pi]
        o_ref[:, sl] = ((yn + bonus) * g_ref[:, sl].astype(F32)).astype(o_ref.dtype)


def rwkv_core(r, k, v, lw, a, g, k_k, k_a, r_k, gn_g, gn_b, batch, seq):
    m, d = r.shape
    c = _pick(seq, RWKV_CHUNK)
    nt = seq // c
    pairs = RWKV_PAIRS_PER_STEP
    w = pairs * LANES
    masks = _tri_masks(2 * c, c, c)
    tok = pl.BlockSpec((c, w), lambda b, p, t: (b * nt + t, p))
    vec = pl.BlockSpec((1, w), lambda b, p, t: (0, p))
    row = lambda x: x.reshape(1, d).astype(F32)
    return pl.pallas_call(
        functools.partial(_rwkv_kernel, pairs=pairs),
        out_shape=jax.ShapeDtypeStruct((m, d), BF16),
        grid=(batch, d // w, nt),
        in_specs=[tok] * 6 + [vec] * 5 + [pl.BlockSpec(masks.shape, lambda b, p, t: (0, 0, 0))],
        out_specs=tok,
        scratch_shapes=[pltpu.VMEM((pairs, LANES, LANES), F32)],
        compiler_params=_params(("parallel", "parallel", "arbitrary")),
        name="rwkv_core",
    )(r, k, v, lw, a, g, row(k_k), row(k_a), row(r_k), row(gn_g), row(gn_b), masks)


def rwkv_layer(h, mu, w_rkv, w0, w1, w2, a0, a1, a2, g1, g2, k_k, k_a, r_k, gn_g, gn_b,
               w_out, batch, seq):
    xm = rwkv_mix(h, mu, seq)
    w_rkv_b = w_rkv.astype(BF16)
    r = matmul(xm[0], w_rkv_b[0], BF16)
    k = matmul(xm[1], w_rkv_b[1], BF16)
    v = matmul(xm[2], w_rkv_b[2], BF16)
    lw = lora(xm[3], w1, w2, w0, "tanh", "logdecay", F32)
    a = lora(xm[4], a1, a2, a0, "none", "sigmoid", F32)
    g = lora(xm[5], g1, g2, jnp.zeros_like(a0), "sigmoid", "none", BF16)
    y = rwkv_core(r, k, v, lw, a, g, k_k, k_a, r_k, gn_g, gn_b, batch, seq)
    return matmul(y, w_out.astype(BF16), F32)


def _lru_kernel(gate_ref, rec_ref, cw_ref, cb_ref, wgx_ref, bgx_ref, wga_ref, bga_ref,
                lam_ref, o_ref, tail_ref, h_ref):
    t = pl.program_id(2)
    c = rec_ref.shape[0]

    @pl.when(t == 0)
    def _():
        tail_ref[...] = jnp.zeros_like(tail_ref)
        h_ref[...] = jnp.zeros_like(h_ref)

    x = rec_ref[...].astype(F32)
    u = _causal_conv(x, tail_ref[...], cw_ref[...]) + cb_ref[...]
    tail_ref[...] = x[c - SUBLANES:]
    ub = u.astype(BF16)
    i_t = _sigmoid(jnp.dot(ub, wgx_ref[...], preferred_element_type=F32) + bgx_ref[...])
    r_t = _sigmoid(jnp.dot(ub, wga_ref[...], preferred_element_type=F32) + bga_ref[...])
    log_a = -LRU_C * r_t * _softplus(-lam_ref[...])
    a = jnp.exp(log_a)
    th = jnp.tanh(log_a)
    b = jnp.sqrt(-2.0 * th / (1.0 - th)) * (i_t * u)

    row = lax.broadcasted_iota(jnp.int32, a.shape, 0)
    sft = 1
    while sft < c:
        keep = row >= sft
        a_s = jnp.where(keep, pltpu.roll(a, sft, axis=0), 1.0)
        b_s = jnp.where(keep, pltpu.roll(b, sft, axis=0), 0.0)
        b = a * b_s + b
        a = a * a_s
        sft *= 2
    hcur = b + a * h_ref[0:1, :]
    h_ref[0:1, :] = hcur[c - 1:c, :]

    gt = gate_ref[...].astype(F32)
    gelu = 0.5 * gt * (1.0 + jnp.tanh(math.sqrt(2.0 / math.pi) * (gt + 0.044715 * gt * gt * gt)))
    o_ref[...] = (hcur * gelu).astype(o_ref.dtype)


def lru_core(proj, conv_w, conv_b, w_gx, b_gx, w_ga, b_ga, lam, batch, seq):
    m = proj.shape[0]
    c = _pick(seq, LRU_TBLOCK)
    nt = seq // c
    nb = LRU_BLOCKS
    w = LRU_BLOCK
    vec = pl.BlockSpec((1, w), lambda b, j, t: (0, j))
    blk = pl.BlockSpec((None, w, w), lambda b, j, t: (j, 0, 0))
    row = lambda x: x.reshape(1, nb * w).astype(F32)
    return pl.pallas_call(
        _lru_kernel,
        out_shape=jax.ShapeDtypeStruct((m, nb * w), BF16),
        grid=(batch, nb, nt),
        in_specs=[pl.BlockSpec((c, w), lambda b, j, t: (b * nt + t, j)),
                  pl.BlockSpec((c, w), lambda b, j, t: (b * nt + t, nb + j)),
                  pl.BlockSpec((CONV_K, w), lambda b, j, t: (0, j)),
                  vec, blk, vec, blk, vec, vec],
        out_specs=pl.BlockSpec((c, w), lambda b, j, t: (b * nt + t, j)),
        scratch_shapes=[pltpu.VMEM((SUBLANES, w), F32), pltpu.VMEM((SUBLANES, w), F32)],
        compiler_params=_params(("parallel", "parallel", "arbitrary")),
        name="lru_core",
    )(proj, proj, conv_w, row(conv_b), w_gx.astype(BF16), row(b_gx), w_ga.astype(BF16),
      row(b_ga), row(lam))


def lru_layer(hb, w_in, conv_w, conv_b, w_gx, b_gx, w_ga, b_ga, lam, w_out, batch, seq):
    proj = matmul(hb, w_in.astype(BF16), BF16)
    y = lru_core(proj, conv_w, conv_b, w_gx, b_gx, w_ga, b_ga, lam, batch, seq)
    return matmul(y, w_out.astype(BF16), F32)


def dense_ffn(hb, w_gu, w_down):
    act = swiglu_up(hb, w_gu.astype(BF16))
    return matmul(act, w_down.astype(BF16), F32)


def moe_ffn(h, hb, router_w, router_b, w_gu, w_down):
    comb = router(h, router_w, router_b)
    act = moe_up(hb, w_gu.astype(BF16), comb)
    ne, fe, d = w_down.shape
    return matmul(act, w_down.astype(BF16).reshape(ne * fe, d), F32)


def kernel(x, ln_g, ln_b, gdn_w_in, gdn_conv_w, gdn_a_log, gdn_dt_bias, gdn_norm_w, gdn_w_out, rwkv_mu, rwkv_w_rkv, rwkv_w0, rwkv_w1, rwkv_w2, rwkv_a0, rwkv_a1, rwkv_a2, rwkv_g1, rwkv_g2, rwkv_k_k, rwkv_k_a, rwkv_r_k, rwkv_gn_g, rwkv_gn_b, rwkv_w_out, lru_w_in, lru_conv_w, lru_conv_b, lru_w_gx, lru_b_gx, lru_w_ga, lru_b_ga, lru_lam, lru_w_out, ffn_w_gu, ffn_w_down, moe_router_w, moe_router_b, moe_w_gu, moe_w_down):
    batch, seq, d = x.shape
    h = x.reshape(batch * seq, d)
    hb = h.astype(BF16)
    for i in range(DEPTH):
        m, s = i % 3, i // 3
        if m == 0:
            y = gdn_layer(hb, gdn_w_in[s], gdn_conv_w[s], gdn_a_log[s], gdn_dt_bias[s],
                          gdn_norm_w[s], gdn_w_out[s], batch, seq)
        elif m == 1:
            y = rwkv_layer(h, rwkv_mu[s], rwkv_w_rkv[s], rwkv_w0[s], rwkv_w1[s], rwkv_w2[s],
                           rwkv_a0[s], rwkv_a1[s], rwkv_a2[s], rwkv_g1[s], rwkv_g2[s],
                           rwkv_k_k[s], rwkv_k_a[s], rwkv_r_k[s], rwkv_gn_g[s], rwkv_gn_b[s],
                           rwkv_w_out[s], batch, seq)
        else:
            y = lru_layer(hb, lru_w_in[s], lru_conv_w[s], lru_conv_b[s], lru_w_gx[s],
                          lru_b_gx[s], lru_w_ga[s], lru_b_ga[s], lru_lam[s], lru_w_out[s],
                          batch, seq)
        h, hb = ln_residual(h, y, ln_g[i, 0], ln_b[i, 0])
        f = i // 2
        if i % 2 == 0:
            y = dense_ffn(hb, ffn_w_gu[f], ffn_w_down[f])
        else:
            y = moe_ffn(h, hb, moe_router_w[f], moe_router_b[f], moe_w_gu[f], moe_w_down[f])
        h, hb = ln_residual(h, y, ln_g[i, 1], ln_b[i, 1])
    return h.reshape(batch, seq, d)
```

```python
import functools
import math

import jax
import jax.numpy as jnp
import numpy as np
from jax import lax
from jax.experimental import pallas as pl
from jax.experimental.pallas import tpu as pltpu

F32 = jnp.float32
BF16 = jnp.bfloat16

D_MODEL = 4096
DEPTH = 4
DN_ALPHA = (2 * DEPTH) ** 0.25
LN_EPS = 1e-5
CONV_K = 4
GDN_DK = 128
GDN_HEADS = D_MODEL // GDN_DK
RWKV_HEAD = 64
RWKV_GN_EPS = 64e-5
LRU_BLOCK = 256
LRU_BLOCKS = D_MODEL // LRU_BLOCK
LRU_C = 8.0
N_EXPERTS = 8

V7X_VMEM_BYTES = 64 * 1024 * 1024
VMEM_LIMIT = V7X_VMEM_BYTES - 8 * 1024 * 1024
LANES = 128
SUBLANES = 8

GDN_CHUNK = 256
RWKV_CHUNK = 64
LRU_TBLOCK = 256
INV_BASE = 16
GDN_HEADS_PER_STEP = 4
MOE_TILE = 512
RWKV_PAIRS_PER_STEP = 16


def _params(sem):
    return pltpu.CompilerParams(dimension_semantics=sem, vmem_limit_bytes=VMEM_LIMIT)


def _dot(a, b):
    return jnp.dot(a.astype(BF16), b.astype(BF16), preferred_element_type=F32)


def _dot_nt(a, b):
    return lax.dot_general(a.astype(BF16), b.astype(BF16), (((1,), (1,)), ((), ())),
                           preferred_element_type=F32)


def _dot_tn(a, b):
    return lax.dot_general(a.astype(BF16), b.astype(BF16), (((0,), (0,)), ((), ())),
                           preferred_element_type=F32)


def _sigmoid(x):
    return 1.0 / (1.0 + jnp.exp(-x))


def _silu(x):
    return x * _sigmoid(x)


def _softplus(x):
    return jnp.maximum(x, 0.0) + jnp.log1p(jnp.exp(-jnp.abs(x)))


def _pick(n, pref):
    t = min(n, pref)
    while n % t:
        t //= 2
    return t


def _mm_kernel(x_ref, w_ref, o_ref, *, nk):
    acc = jnp.dot(x_ref[...], w_ref[...], preferred_element_type=F32)
    if nk == 1:
        o_ref[...] = acc.astype(o_ref.dtype)
    else:
        k = pl.program_id(2)

        @pl.when(k == 0)
        def _():
            o_ref[...] = acc

        @pl.when(k > 0)
        def _():
            o_ref[...] += acc


def matmul(x, w, out_dtype, xi=None, wi=None, n=None, bm=1024, bn=1024, bk=4096):
    m, kdim = x.shape[-2:]
    n = w.shape[-1] if n is None else n
    bm, bn, bk = _pick(m, bm), _pick(n, bn), _pick(kdim, bk)
    nk = kdim // bk
    if nk > 1:
        assert out_dtype == F32
    if xi is None:
        x_spec = pl.BlockSpec((bm, bk), lambda i, j, k: (i, k))
    else:
        x_spec = pl.BlockSpec((None, bm, bk), lambda i, j, k: (xi, i, k))
    if wi is None:
        w_spec = pl.BlockSpec((bk, bn), lambda i, j, k: (k, j))
    else:
        w_spec = pl.BlockSpec((None, bk, bn), lambda i, j, k: (wi, k, j))
    return pl.pallas_call(
        functools.partial(_mm_kernel, nk=nk),
        out_shape=jax.ShapeDtypeStruct((m, n), out_dtype),
        grid=(m // bm, n // bn, nk),
        in_specs=[x_spec, w_spec],
        out_specs=pl.BlockSpec((bm, bn), lambda i, j, k: (i, j)),
        compiler_params=_params(("parallel", "parallel", "arbitrary")),
        name="matmul",
    )(x, w)


def _swiglu_kernel(x_ref, wg_ref, wu_ref, o_ref):
    x = x_ref[...]
    g = jnp.dot(x, wg_ref[...], preferred_element_type=F32)
    u = jnp.dot(x, wu_ref[...], preferred_element_type=F32)
    o_ref[...] = (_silu(g) * u).astype(o_ref.dtype)


def swiglu_up(x, w_gu, bm=1024, bn=512):
    m, d = x.shape
    f = w_gu.shape[1] // 2
    bm, bn = _pick(m, bm), _pick(f, bn)
    nj = f // bn
    return pl.pallas_call(
        _swiglu_kernel,
        out_shape=jax.ShapeDtypeStruct((m, f), BF16),
        grid=(m // bm, nj),
        in_specs=[pl.BlockSpec((bm, d), lambda i, j: (i, 0)),
                  pl.BlockSpec((d, bn), lambda i, j: (0, j)),
                  pl.BlockSpec((d, bn), lambda i, j: (0, j + nj))],
        out_specs=pl.BlockSpec((bm, bn), lambda i, j: (i, j)),
        compiler_params=_params(("parallel", "parallel")),
        name="swiglu_up",
    )(x, w_gu, w_gu)


def _gather_kernel(idx_ref, src_ref, dst_ref, sem, *, rps):
    base = pl.program_id(0) * rps

    def row_copy(src_row, dst_row):
        return pltpu.make_async_copy(src_ref.at[pl.ds(src_row, 1)],
                                     dst_ref.at[pl.ds(dst_row, 1)], sem)

    def start(r, carry):
        row_copy(idx_ref[0, 0, r], base + r).start()
        return carry

    def wait(r, carry):
        row_copy(0, base).wait()
        return carry

    lax.fori_loop(0, rps, start, 0)
    lax.fori_loop(0, rps, wait, 0)


def row_gather(src, idx, rps=512):
    n = idx.shape[0]
    rps = _pick(n, rps)
    nb = n // rps
    return pl.pallas_call(
        functools.partial(_gather_kernel, rps=rps),
        out_shape=jax.ShapeDtypeStruct((n, src.shape[1]), src.dtype),
        grid=(nb,),
        in_specs=[pl.BlockSpec((1, 1, rps), lambda i: (i, 0, 0), memory_space=pltpu.SMEM),
                  pl.BlockSpec(memory_space=pl.ANY)],
        out_specs=pl.BlockSpec(memory_space=pl.ANY),
        scratch_shapes=[pltpu.SemaphoreType.DMA(())],
        compiler_params=_params(("arbitrary",)),
        name="row_gather",
    )(idx.reshape(nb, 1, rps), src)


def _moe_up_kernel(te_ref, x_ref, wg_ref, wu_ref, o_ref):
    x = x_ref[...].astype(BF16)
    g = jnp.dot(x, wg_ref[...], preferred_element_type=F32)
    u = jnp.dot(x, wu_ref[...], preferred_element_type=F32)
    o_ref[...] = (_silu(g) * u).astype(o_ref.dtype)


def moe_up(xs, w_gu, tile_expert, bn=768):
    p, d = xs.shape
    f = w_gu.shape[2] // 2
    bm = MOE_TILE
    bn = _pick(f, bn)
    nj = f // bn
    grid_spec = pltpu.PrefetchScalarGridSpec(
        num_scalar_prefetch=1,
        grid=(nj, p // bm),
        in_specs=[pl.BlockSpec((bm, d), lambda j, i, te: (i, 0)),
                  pl.BlockSpec((None, d, bn), lambda j, i, te: (te[i], 0, j)),
                  pl.BlockSpec((None, d, bn), lambda j, i, te: (te[i], 0, j + nj))],
        out_specs=pl.BlockSpec((bm, bn), lambda j, i, te: (i, j)))
    return pl.pallas_call(
        _moe_up_kernel,
        out_shape=jax.ShapeDtypeStruct((p, f), BF16),
        grid_spec=grid_spec,
        compiler_params=_params(("parallel", "parallel")),
        name="moe_up",
    )(tile_expert, xs, w_gu, w_gu)


def _moe_down_kernel(te_ref, a_ref, w_ref, o_ref):
    o_ref[...] = jnp.dot(a_ref[...], w_ref[...], preferred_element_type=F32)


def moe_down(act, w_down, tile_expert, bn=2048):
    p, f = act.shape
    d = w_down.shape[2]
    bm = MOE_TILE
    bn = _pick(d, bn)
    grid_spec = pltpu.PrefetchScalarGridSpec(
        num_scalar_prefetch=1,
        grid=(d // bn, p // bm),
        in_specs=[pl.BlockSpec((bm, f), lambda j, i, te: (i, 0)),
                  pl.BlockSpec((None, f, bn), lambda j, i, te: (te[i], 0, j))],
        out_specs=pl.BlockSpec((bm, bn), lambda j, i, te: (i, j)))
    return pl.pallas_call(
        _moe_down_kernel,
        out_shape=jax.ShapeDtypeStruct((p, d), F32),
        grid_spec=grid_spec,
        compiler_params=_params(("parallel", "parallel")),
        name="moe_down",
    )(tile_expert, act, w_down)


def _moe_plan(ri, bm):
    m = ri.shape[0]
    na = 2 * m
    n_slots = na + N_EXPERTS * bm
    n_tiles = n_slots // bm
    ids = jnp.arange(N_EXPERTS, dtype=jnp.int32)[None, :]
    e_flat = ri[:, N_EXPERTS:N_EXPERTS + 2].astype(jnp.int32).reshape(na)
    onehot = (e_flat[:, None] == ids).astype(jnp.int32)
    csum = jnp.cumsum(onehot, axis=0)
    counts = csum[-1]
    padded = (counts + bm - 1) // bm * bm
    ends = jnp.cumsum(padded)
    offs = ends - padded
    slot_a = jnp.sum(onehot * (offs[None, :] + csum - onehot), axis=1)
    tile_start = jnp.arange(n_tiles, dtype=jnp.int32) * bm
    tile_expert = jnp.minimum(jnp.sum((tile_start[:, None] >= ends[None, :]).astype(jnp.int32),
                                      axis=1), N_EXPERTS - 1).astype(jnp.int32)
    npad = n_slots - na
    pad = padded - counts
    pad_ends = jnp.cumsum(pad)
    pi = jnp.arange(npad, dtype=jnp.int32)
    pe = jnp.sum((pi[:, None] >= pad_ends[None, :]).astype(jnp.int32), axis=1)
    poh = (pe[:, None] == ids).astype(jnp.int32)
    in_group = jnp.sum(poh * (offs + counts - (pad_ends - pad))[None, :], axis=1) + pi
    tail = ends[-1] + (pi - pad_ends[-1])
    pad_slot = jnp.where(pe < N_EXPERTS, in_group, tail)
    keys = jnp.concatenate([slot_a, pad_slot])
    vals = jnp.concatenate([jnp.arange(na, dtype=jnp.int32) // 2, jnp.zeros((npad,), jnp.int32)])
    _, slot_token = lax.sort_key_val(keys, vals)
    return slot_token, slot_a, tile_expert


def _router_kernel(x_ref, w_ref, b_ref, o_ref):
    logits = jnp.dot(x_ref[...], w_ref[...], preferred_element_type=F32,
                     precision=lax.Precision.HIGHEST) + b_ref[...]
    lane = lax.broadcasted_iota(jnp.int32, logits.shape, 1)
    neg = jnp.float32(-jnp.inf)
    logits = jnp.where(lane < N_EXPERTS, logits, neg)
    m1 = jnp.max(logits, axis=1, keepdims=True)
    i1 = jnp.min(jnp.where(logits == m1, lane, LANES), axis=1, keepdims=True)
    sel1 = lane == i1
    rest = jnp.where(sel1, neg, logits)
    m2 = jnp.max(rest, axis=1, keepdims=True)
    i2 = jnp.min(jnp.where(rest == m2, lane, LANES), axis=1, keepdims=True)
    sel2 = lane == i2
    e2 = jnp.exp(m2 - m1)
    p1 = 1.0 / (1.0 + e2)
    p2 = e2 / (1.0 + e2)
    out = jnp.where(sel1, p1, 0.0) + jnp.where(sel2, p2, 0.0)
    out = jnp.where(lane == N_EXPERTS, i1.astype(F32), out)
    out = jnp.where(lane == N_EXPERTS + 1, i2.astype(F32), out)
    out = jnp.where(lane == N_EXPERTS + 2, p1, out)
    o_ref[...] = jnp.where(lane == N_EXPERTS + 3, p2, out)


def router(x, rw, rb, bm=512):
    m, d = x.shape
    bm = _pick(m, bm)
    rw_p = jnp.zeros((d, LANES), F32).at[:, :N_EXPERTS].set(rw.astype(F32))
    rb_p = jnp.zeros((1, LANES), F32).at[0, :N_EXPERTS].set(rb.astype(F32))
    return pl.pallas_call(
        _router_kernel,
        out_shape=jax.ShapeDtypeStruct((m, LANES), F32),
        grid=(m // bm,),
        in_specs=[pl.BlockSpec((bm, d), lambda i: (i, 0)),
                  pl.BlockSpec((d, LANES), lambda i: (0, 0)),
                  pl.BlockSpec((1, LANES), lambda i: (0, 0))],
        out_specs=pl.BlockSpec((bm, LANES), lambda i: (i, 0)),
        compiler_params=_params(("parallel",)),
        name="router",
    )(x, rw_p, rb_p)


def _ln_kernel(h_ref, y_ref, g_ref, b_ref, o_ref, ob_ref):
    x = DN_ALPHA * h_ref[...] + y_ref[...]
    mu = jnp.mean(x, axis=-1, keepdims=True)
    xc = x - mu
    var = jnp.mean(xc * xc, axis=-1, keepdims=True)
    o = xc * lax.rsqrt(var + LN_EPS) * g_ref[...] + b_ref[...]
    o_ref[...] = o
    ob_ref[...] = o.astype(BF16)


def _ln_moe_kernel(h_ref, y_ref, ri_ref, g_ref, b_ref, o_ref, ob_ref):
    ri = ri_ref[...]
    lane = lax.broadcasted_iota(jnp.int32, ri.shape, 1)
    p1 = jnp.sum(jnp.where(lane == N_EXPERTS + 2, ri, 0.0), axis=1, keepdims=True)
    p2 = jnp.sum(jnp.where(lane == N_EXPERTS + 3, ri, 0.0), axis=1, keepdims=True)
    x = DN_ALPHA * h_ref[...] + (p1 * y_ref[0] + p2 * y_ref[1])
    mu = jnp.mean(x, axis=-1, keepdims=True)
    xc = x - mu
    var = jnp.mean(xc * xc, axis=-1, keepdims=True)
    o = xc * lax.rsqrt(var + LN_EPS) * g_ref[...] + b_ref[...]
    o_ref[...] = o
    ob_ref[...] = o.astype(BF16)


def ln_residual_moe(h, y2, ri, g, b, tm=256):
    m, d = h.shape
    tm = _pick(m, tm)
    row = pl.BlockSpec((tm, d), lambda i: (i, 0))
    vec = pl.BlockSpec((1, d), lambda i: (0, 0))
    return pl.pallas_call(
        _ln_moe_kernel,
        out_shape=(jax.ShapeDtypeStruct((m, d), F32), jax.ShapeDtypeStruct((m, d), BF16)),
        grid=(m // tm,),
        in_specs=[row, pl.BlockSpec((2, tm, d), lambda i: (0, i, 0)),
                  pl.BlockSpec((tm, LANES), lambda i: (i, 0)), vec, vec],
        out_specs=(row, row),
        compiler_params=_params(("parallel",)),
        name="ln_residual_moe",
    )(h, y2, ri, g.reshape(1, d), b.reshape(1, d))


def ln_residual(h, y, g, b, tm=256):
    m, d = h.shape
    tm = _pick(m, tm)
    row = pl.BlockSpec((tm, d), lambda i: (i, 0))
    vec = pl.BlockSpec((1, d), lambda i: (0, 0))
    return pl.pallas_call(
        _ln_kernel,
        out_shape=(jax.ShapeDtypeStruct((m, d), F32), jax.ShapeDtypeStruct((m, d), BF16)),
        grid=(m // tm,),
        in_specs=[row, row, vec, vec],
        out_specs=(row, row),
        compiler_params=_params(("parallel",)),
        name="ln_residual",
    )(h, y, g.reshape(1, d), b.reshape(1, d))


def _causal_conv(x, tail, w):
    c = x.shape[0]
    xs = jnp.concatenate([tail, x], axis=0)
    y = xs[SUBLANES:] * w[CONV_K - 1:CONV_K]
    for j in range(CONV_K - 1):
        y = y + pltpu.roll(xs, CONV_K - 1 - j, axis=0)[SUBLANES:SUBLANES + c] * w[j:j + 1]
    return y


def _tri_masks(n, period, top):
    i = np.arange(n)[:, None]
    j = np.arange(n)[None, :]
    ms = [i == j, (i // INV_BASE) == (j // INV_BASE)]
    size = INV_BASE
    while size < top:
        bi, bj = i // size, j // size
        ms.append((bi // 2 == bj // 2) & (bi % 2 == 1) & (bj % 2 == 0))
        size *= 2
    ti, tj = i % period, j % period
    ms += [ti >= tj, ti > tj, ti <= tj]
    return jnp.asarray(np.stack(ms).astype(np.float32))


def _n_merge_levels(top):
    return (top // INV_BASE).bit_length() - 1


def _unit_lower_inverse(a_strict, m_ref, top):
    ps = [-a * m_ref[1] for a in a_strict]
    xs = [m_ref[0] + p for p in ps]
    k = 2
    while k < INV_BASE:
        ps = [_dot(p, p) for p in ps]
        xs = [x + _dot(x, p) for x, p in zip(xs, ps)]
        k *= 2
    for lvl in range(_n_merge_levels(top)):
        ys = [_dot(a * m_ref[2 + lvl], x) for a, x in zip(a_strict, xs)]
        xs = [x - _dot(x, y) for x, y in zip(xs, ys)]
    return xs


def _gdn_kernel(alog_ref, dtb_ref, q_ref, k_ref, v_ref, z_ref, ba_ref,
                cwq_ref, cwk_ref, cwv_ref, nw_ref, m_ref, o_ref, s_ref, tail_ref, *, heads):
    hg = pl.program_id(1)
    t = pl.program_id(2)
    c = q_ref.shape[0]
    nl = _n_merge_levels(c)
    causal, strict, upper = m_ref[2 + nl], m_ref[3 + nl], m_ref[4 + nl]
    eye = m_ref[0]

    @pl.when(t == 0)
    def _():
        s_ref[...] = jnp.zeros_like(s_ref)
        tail_ref[...] = jnp.zeros_like(tail_ref)

    ba = ba_ref[...]
    lane = lax.broadcasted_iota(jnp.int32, ba.shape, 1)

    hs = range(heads)
    sls = [slice(gi * GDN_DK, (gi + 1) * GDN_DK) for gi in hs]

    def conv_silu(x_ref, cw_ref, slot, sl):
        x = x_ref[:, sl].astype(F32)
        y = _causal_conv(x, tail_ref[slot, :, sl], cw_ref[:, sl])
        tail_ref[slot, :, sl] = x[c - SUBLANES:]
        return _silu(y)

    qs = [conv_silu(q_ref, cwq_ref, 0, sl) for sl in sls]
    ks = [conv_silu(k_ref, cwk_ref, 1, sl) for sl in sls]
    vs = [conv_silu(v_ref, cwv_ref, 2, sl) for sl in sls]
    qs = [q * lax.rsqrt(jnp.sum(q * q, axis=-1, keepdims=True) + 1e-6) * (GDN_DK ** -0.5)
          for q in qs]
    ks = [k * lax.rsqrt(jnp.sum(k * k, axis=-1, keepdims=True) + 1e-6) for k in ks]

    betas, g_cols, gammas = [], [], []
    for gi in hs:
        h = hg * heads + gi
        b_col = jnp.sum(jnp.where(lane == h, ba, 0.0), axis=1, keepdims=True)
        a_col = jnp.sum(jnp.where(lane == h + GDN_HEADS, ba, 0.0), axis=1, keepdims=True)
        betas.append(_sigmoid(b_col))
        g = -jnp.exp(alog_ref[h]) * _softplus(a_col + dtb_ref[h])
        g_row = jnp.sum(g * upper, axis=0, keepdims=True)
        g_col = jnp.sum(g_row * eye, axis=1, keepdims=True)
        g_cols.append(g_col)
        gammas.append(jnp.exp((g_col - g_row) * causal) * causal)

    kbs = [k * b for k, b in zip(ks, betas)]
    a_mats = [_dot_nt(kb, k) * (gm * strict) for kb, k, gm in zip(kbs, ks, gammas)]
    qks = [_dot_nt(q, k) * gm for q, k, gm in zip(qs, ks, gammas)]
    t_invs = _unit_lower_inverse(a_mats, m_ref, c)

    egs = [jnp.exp(gc) for gc in g_cols]
    uws = [_dot(ti, jnp.concatenate([v * b, kb * eg], axis=1))
           for ti, v, b, kb, eg in zip(t_invs, vs, betas, kbs, egs)]
    ss = [s_ref[gi] for gi in hs]
    ws_qs = [_dot(jnp.concatenate([uw[:, GDN_DK:], q * eg], axis=0), s)
             for uw, q, eg, s in zip(uws, qs, egs, ss)]
    v_news = [uw[:, :GDN_DK] - wq[:c] for uw, wq in zip(uws, ws_qs)]
    os_ = [wq[c:] + _dot(qk, vn) for wq, qk, vn in zip(ws_qs, qks, v_news)]
    for gi in hs:
        g_last = g_cols[gi][c - 1:c, :]
        s_ref[gi] = (ss[gi] * jnp.exp(g_last)
                     + _dot_tn(ks[gi] * jnp.exp(g_last - g_cols[gi]), v_news[gi]))
    for gi in hs:
        o = os_[gi]
        o = o * lax.rsqrt(jnp.mean(o * o, axis=-1, keepdims=True) + 1e-6) * nw_ref[...]
        o_ref[:, sls[gi]] = (o * _silu(z_ref[:, sls[gi]].astype(F32))).astype(o_ref.dtype)


def gdn_core(qkvz, ba, conv_w, a_log, dt_bias, norm_w, batch, seq):
    m = qkvz.shape[0]
    c = _pick(seq, GDN_CHUNK)
    nt = seq // c
    nh = GDN_HEADS
    g = GDN_HEADS_PER_STEP
    ng = nh // g
    w = g * GDN_DK
    masks = _tri_masks(c, c, c)

    def col(off):
        return pl.BlockSpec((c, w), lambda b, h, t: (b * nt + t, off * ng + h))

    def cw(off):
        return pl.BlockSpec((CONV_K, w), lambda b, h, t: (0, off * ng + h))

    smem = pl.BlockSpec(memory_space=pltpu.SMEM)
    return pl.pallas_call(
        functools.partial(_gdn_kernel, heads=g),
        out_shape=jax.ShapeDtypeStruct((m, D_MODEL), BF16),
        grid=(batch, ng, nt),
        in_specs=[smem, smem, col(0), col(1), col(2), col(3),
                  pl.BlockSpec((c, 2 * nh), lambda b, h, t: (b * nt + t, 0)),
                  cw(0), cw(1), cw(2),
                  pl.BlockSpec((1, GDN_DK), lambda b, h, t: (0, 0)),
                  pl.BlockSpec(masks.shape, lambda b, h, t: (0, 0, 0))],
        out_specs=pl.BlockSpec((c, w), lambda b, h, t: (b * nt + t, h)),
        scratch_shapes=[pltpu.VMEM((g, GDN_DK, GDN_DK), F32),
                        pltpu.VMEM((3, SUBLANES, w), F32)],
        compiler_params=_params(("parallel", "parallel", "arbitrary")),
        name="gdn_core",
    )(a_log.astype(F32), dt_bias.astype(F32), qkvz, qkvz, qkvz, qkvz, ba,
      conv_w, conv_w, conv_w, norm_w.reshape(1, GDN_DK), masks)


def gdn_layer(hb, w_in, conv_w, a_log, dt_bias, norm_w, w_out, batch, seq):
    d4 = 4 * D_MODEL
    w_in_b = w_in.astype(BF16)
    qkvz = matmul(hb, w_in_b, BF16, n=d4)
    ba = matmul(hb, w_in_b[:, d4:], F32)
    o = gdn_core(qkvz, ba, conv_w, a_log, dt_bias, norm_w, batch, seq)
    return matmul(o, w_out.astype(BF16), F32)


def _rwkv_mix_kernel(x_ref, halo_ref, mu_ref, o_ref, *, seq):
    i = pl.program_id(0)
    tm = x_ref.shape[0]
    x = x_ref[...]
    prev = halo_ref[SUBLANES - 1:SUBLANES, :]
    prev = jnp.where((i * tm) % seq == 0, jnp.zeros_like(prev), prev)
    row = lax.broadcasted_iota(jnp.int32, x.shape, 0)
    xprev = jnp.where(row == 0, prev, pltpu.roll(x, 1, axis=0))
    xx = xprev - x
    for p in range(6):
        o_ref[p] = (x + xx * mu_ref[p:p + 1, :]).astype(o_ref.dtype)


def rwkv_mix(h, mu, seq, tm=256):
    m, d = h.shape
    tm = _pick(seq, tm)
    r8 = tm // SUBLANES
    return pl.pallas_call(
        functools.partial(_rwkv_mix_kernel, seq=seq),
        out_shape=jax.ShapeDtypeStruct((6, m, d), BF16),
        grid=(m // tm,),
        in_specs=[pl.BlockSpec((tm, d), lambda i: (i, 0)),
                  pl.BlockSpec((SUBLANES, d), lambda i: (jnp.maximum(i * r8 - 1, 0), 0)),
                  pl.BlockSpec((6, d), lambda i: (0, 0))],
        out_specs=pl.BlockSpec((6, tm, d), lambda i: (0, i, 0)),
        compiler_params=_params(("parallel",)),
        name="rwkv_mix",
    )(h, h, mu)


def _lora_kernel(x_ref, a_ref, b_ref, bias_ref, o_ref, *, act, epi):
    t = jnp.dot(x_ref[...], a_ref[...], preferred_element_type=F32)
    if act == "tanh":
        t = jnp.tanh(t)
    elif act == "sigmoid":
        t = _sigmoid(t)
    y = jnp.dot(t.astype(BF16), b_ref[...], preferred_element_type=F32) + bias_ref[...]
    if epi == "logdecay":
        y = -jnp.exp(-_softplus(-y) - 0.5)
    elif epi == "sigmoid":
        y = _sigmoid(y)
    o_ref[...] = y.astype(o_ref.dtype)


def lora(x, xi, a, b, bias, act, epi, out_dtype, tm=512):
    _, m, d = x.shape
    r = a.shape[1]
    n = b.shape[1]
    tm = _pick(m, tm)
    return pl.pallas_call(
        functools.partial(_lora_kernel, act=act, epi=epi),
        out_shape=jax.ShapeDtypeStruct((m, n), out_dtype),
        grid=(m // tm,),
        in_specs=[pl.BlockSpec((None, tm, d), lambda i: (xi, i, 0)),
                  pl.BlockSpec((d, r), lambda i: (0, 0)),
                  pl.BlockSpec((r, n), lambda i: (0, 0)),
                  pl.BlockSpec((1, n), lambda i: (0, 0))],
        out_specs=pl.BlockSpec((tm, n), lambda i: (i, 0)),
        compiler_params=_params(("parallel",)),
        name="lora_" + epi,
    )(x, a.astype(BF16), b.astype(BF16), bias.reshape(1, n).astype(F32))


def _group_sum(x, lo):
    s0 = jnp.sum(jnp.where(lo, x, 0.0), axis=1, keepdims=True)
    s1 = jnp.sum(jnp.where(lo, 0.0, x), axis=1, keepdims=True)
    return jnp.where(lo, s0, s1)


def _rwkv_kernel(r_ref, k_ref, v_ref, lw_ref, a_ref, g_ref, kk_ref, ka_ref, rk_ref,
                 gng_ref, gnb_ref, m_ref, o_ref, s_ref, *, pairs):
    t = pl.program_id(2)
    c = r_ref.shape[0]
    c2 = 2 * c
    nl = _n_merge_levels(c)
    incl, strict = m_ref[2 + nl], m_ref[3 + nl]

    @pl.when(t == 0)
    def _():
        s_ref[...] = jnp.zeros_like(s_ref)

    lane = lax.broadcasted_iota(jnp.int32, (c, LANES), 1)
    row = lax.broadcasted_iota(jnp.int32, (c, LANES), 0)
    lo = lane < RWKV_HEAD

    def stack(x):
        return jnp.concatenate([jnp.where(lo, x, 0.0), jnp.where(lo, 0.0, x)], axis=0)

    ps = range(pairs)
    sls = [slice(pi * LANES, (pi + 1) * LANES) for pi in ps]
    rs = [r_ref[:, sl].astype(F32) for sl in sls]
    ks = [k_ref[:, sl].astype(F32) for sl in sls]
    vs = [v_ref[:, sl].astype(F32) for sl in sls]
    lws = [lw_ref[:, sl] for sl in sls]
    a_s = [a_ref[:, sl] for sl in sls]

    kks = [k * kk_ref[:, sl] for k, sl in zip(ks, sls)]
    kks = [kk * lax.rsqrt(_group_sum(kk * kk, lo) + 1e-6) for kk in kks]
    k2s = [k * (1.0 + (a - 1.0) * ka_ref[:, sl]) for k, a, sl in zip(ks, a_s, sls)]
    kkas = [kk * a for kk, a in zip(kks, a_s)]

    cls = lws
    sft = 1
    while sft < c:
        cls = [cl + jnp.where(row >= sft, pltpu.roll(cl, sft, axis=0), 0.0) for cl in cls]
        sft *= 2

    krs = [jnp.concatenate([stack(jnp.exp(cl - lw) * kk), stack(jnp.exp(cl) * r)], axis=0)
           for cl, lw, kk, r in zip(cls, lws, kks, rs)]
    aks = [jnp.concatenate([stack(jnp.exp(-cl) * kka), stack(jnp.exp(-cl) * k2)], axis=0)
           for cl, kka, k2 in zip(cls, kkas, k2s)]
    vss = [stack(v) for v in vs]
    scs = [_dot_nt(kr, ak) for kr, ak in zip(krs, aks)]
    t_invs = _unit_lower_inverse([sc[:c2, :c2] * strict for sc in scs], m_ref, c)

    ss = [s_ref[pi] for pi in ps]
    kz_rzs = [_dot_nt(kr, s) for kr, s in zip(krs, ss)]
    avs = [_dot(sc[:c2, c2:] * strict, v) for sc, v in zip(scs, vss)]
    us = [_dot(ti, -kz[:c2] - av) for ti, kz, av in zip(t_invs, kz_rzs, avs)]
    uvs = [jnp.concatenate([u, v], axis=0) for u, v in zip(us, vss)]
    yss = [kz[c2:] + _dot(sc[c2:, :] * jnp.concatenate([incl, incl], axis=1), uv)
           for kz, sc, uv in zip(kz_rzs, scs, uvs)]
    for pi in ps:
        cl_last = cls[pi][c - 1:c, :]
        dec = jnp.exp(cl_last - cls[pi])
        ak2 = jnp.concatenate([stack(dec * kkas[pi]), stack(dec * k2s[pi])], axis=0)
        s_ref[pi] = ss[pi] * jnp.exp(cl_last) + _dot_tn(uvs[pi], ak2)

    inv_n = 1.0 / RWKV_HEAD
    for pi in ps:
        sl = sls[pi]
        y = yss[pi][:c] + yss[pi][c:]
        mean = _group_sum(y, lo) * inv_n
        yc = y - mean
        var = _group_sum(yc * yc, lo) * inv_n
        yn = yc * lax.rsqrt(var + RWKV_GN_EPS) * gng_ref[:, sl] + gnb_ref[:, sl]
        bonus = _group_sum(rs[pi] * k2s[pi] * rk_ref[:, sl], lo) * vs[pi]
        o_ref[:, sl] = ((yn + bonus) * g_ref[:, sl].astype(F32)).astype(o_ref.dtype)


def rwkv_core(r, k, v, lw, a, g, k_k, k_a, r_k, gn_g, gn_b, batch, seq):
    m, d = r.shape
    c = _pick(seq, RWKV_CHUNK)
    nt = seq // c
    pairs = RWKV_PAIRS_PER_STEP
    w = pairs * LANES
    masks = _tri_masks(2 * c, c, c)
    tok = pl.BlockSpec((c, w), lambda b, p, t: (b * nt + t, p))
    vec = pl.BlockSpec((1, w), lambda b, p, t: (0, p))
    row = lambda x: x.reshape(1, d).astype(F32)
    return pl.pallas_call(
        functools.partial(_rwkv_kernel, pairs=pairs),
        out_shape=jax.ShapeDtypeStruct((m, d), BF16),
        grid=(batch, d // w, nt),
        in_specs=[tok] * 6 + [vec] * 5 + [pl.BlockSpec(masks.shape, lambda b, p, t: (0, 0, 0))],
        out_specs=tok,
        scratch_shapes=[pltpu.VMEM((pairs, LANES, LANES), F32)],
        compiler_params=_params(("parallel", "parallel", "arbitrary")),
        name="rwkv_core",
    )(r, k, v, lw, a, g, row(k_k), row(k_a), row(r_k), row(gn_g), row(gn_b), masks)


def rwkv_layer(h, mu, w_rkv, w0, w1, w2, a0, a1, a2, g1, g2, k_k, k_a, r_k, gn_g, gn_b,
               w_out, batch, seq):
    xm = rwkv_mix(h, mu, seq)
    w_rkv_b = w_rkv.astype(BF16)
    r = matmul(xm, w_rkv_b, BF16, xi=0, wi=0)
    k = matmul(xm, w_rkv_b, BF16, xi=1, wi=1)
    v = matmul(xm, w_rkv_b, BF16, xi=2, wi=2)
    lw = lora(xm, 3, w1, w2, w0, "tanh", "logdecay", F32)
    a = lora(xm, 4, a1, a2, a0, "none", "sigmoid", F32)
    g = lora(xm, 5, g1, g2, jnp.zeros_like(a0), "sigmoid", "none", BF16)
    y = rwkv_core(r, k, v, lw, a, g, k_k, k_a, r_k, gn_g, gn_b, batch, seq)
    return matmul(y, w_out.astype(BF16), F32)


def _lru_kernel(gate_ref, rec_ref, cw_ref, cb_ref, wgx_ref, bgx_ref, wga_ref, bga_ref,
                lam_ref, o_ref, tail_ref, h_ref):
    t = pl.program_id(2)
    c = rec_ref.shape[0]

    @pl.when(t == 0)
    def _():
        tail_ref[...] = jnp.zeros_like(tail_ref)
        h_ref[...] = jnp.zeros_like(h_ref)

    x = rec_ref[...].astype(F32)
    u = _causal_conv(x, tail_ref[...], cw_ref[...]) + cb_ref[...]
    tail_ref[...] = x[c - SUBLANES:]
    ub = u.astype(BF16)
    i_t = _sigmoid(jnp.dot(ub, wgx_ref[...], preferred_element_type=F32) + bgx_ref[...])
    r_t = _sigmoid(jnp.dot(ub, wga_ref[...], preferred_element_type=F32) + bga_ref[...])
    log_a = -LRU_C * r_t * _softplus(-lam_ref[...])
    a = jnp.exp(log_a)
    th = jnp.tanh(log_a)
    b = jnp.sqrt(-2.0 * th / (1.0 - th)) * (i_t * u)

    row = lax.broadcasted_iota(jnp.int32, a.shape, 0)
    sft = 1
    while sft < c:
        keep = row >= sft
        a_s = jnp.where(keep, pltpu.roll(a, sft, axis=0), 1.0)
        b_s = jnp.where(keep, pltpu.roll(b, sft, axis=0), 0.0)
        b = a * b_s + b
        a = a * a_s
        sft *= 2
    hcur = b + a * h_ref[0:1, :]
    h_ref[0:1, :] = hcur[c - 1:c, :]

    gt = gate_ref[...].astype(F32)
    gelu = 0.5 * gt * (1.0 + jnp.tanh(math.sqrt(2.0 / math.pi) * (gt + 0.044715 * gt * gt * gt)))
    o_ref[...] = (hcur * gelu).astype(o_ref.dtype)


def lru_core(proj, conv_w, conv_b, w_gx, b_gx, w_ga, b_ga, lam, batch, seq):
    m = proj.shape[0]
    c = _pick(seq, LRU_TBLOCK)
    nt = seq // c
    nb = LRU_BLOCKS
    w = LRU_BLOCK
    vec = pl.BlockSpec((1, w), lambda b, j, t: (0, j))
    blk = pl.BlockSpec((None, w, w), lambda b, j, t: (j, 0, 0))
    row = lambda x: x.reshape(1, nb * w).astype(F32)
    return pl.pallas_call(
        _lru_kernel,
        out_shape=jax.ShapeDtypeStruct((m, nb * w), BF16),
        grid=(batch, nb, nt),
        in_specs=[pl.BlockSpec((c, w), lambda b, j, t: (b * nt + t, j)),
                  pl.BlockSpec((c, w), lambda b, j, t: (b * nt + t, nb + j)),
                  pl.BlockSpec((CONV_K, w), lambda b, j, t: (0, j)),
                  vec, blk, vec, blk, vec, vec],
        out_specs=pl.BlockSpec((c, w), lambda b, j, t: (b * nt + t, j)),
        scratch_shapes=[pltpu.VMEM((SUBLANES, w), F32), pltpu.VMEM((SUBLANES, w), F32)],
        compiler_params=_params(("parallel", "parallel", "arbitrary")),
        name="lru_core",
    )(proj, proj, conv_w, row(conv_b), w_gx.astype(BF16), row(b_gx), w_ga.astype(BF16),
      row(b_ga), row(lam))


def lru_layer(hb, w_in, conv_w, conv_b, w_gx, b_gx, w_ga, b_ga, lam, w_out, batch, seq):
    proj = matmul(hb, w_in.astype(BF16), BF16)
    y = lru_core(proj, conv_w, conv_b, w_gx, b_gx, w_ga, b_ga, lam, batch, seq)
    return matmul(y, w_out.astype(BF16), F32)


def dense_ffn(hb, w_gu, w_down):
    act = swiglu_up(hb, w_gu.astype(BF16))
    return matmul(act, w_down.astype(BF16), F32)


def moe_ffn(h, router_w, router_b, w_gu, w_down):
    m, d = h.shape
    ri = router(h, router_w, router_b)
    slot_token, slot_a, tile_expert = _moe_plan(ri, MOE_TILE)
    xs = row_gather(h, slot_token)
    act = moe_up(xs, w_gu.astype(BF16), tile_expert)
    ys = moe_down(act, w_down.astype(BF16), tile_expert)
    y2 = row_gather(ys, slot_a.reshape(m, 2).T.reshape(2 * m))
    return y2.reshape(2, m, d), ri


def kernel(x, ln_g, ln_b, gdn_w_in, gdn_conv_w, gdn_a_log, gdn_dt_bias, gdn_norm_w, gdn_w_out, rwkv_mu, rwkv_w_rkv, rwkv_w0, rwkv_w1, rwkv_w2, rwkv_a0, rwkv_a1, rwkv_a2, rwkv_g1, rwkv_g2, rwkv_k_k, rwkv_k_a, rwkv_r_k, rwkv_gn_g, rwkv_gn_b, rwkv_w_out, lru_w_in, lru_conv_w, lru_conv_b, lru_w_gx, lru_b_gx, lru_w_ga, lru_b_ga, lru_lam, lru_w_out, ffn_w_gu, ffn_w_down, moe_router_w, moe_router_b, moe_w_gu, moe_w_down):
    batch, seq, d = x.shape
    h = x.reshape(batch * seq, d)
    hb = h.astype(BF16)
    for i in range(DEPTH):
        m, s = i % 3, i // 3
        if m == 0:
            y = gdn_layer(hb, gdn_w_in[s], gdn_conv_w[s], gdn_a_log[s], gdn_dt_bias[s],
                          gdn_norm_w[s], gdn_w_out[s], batch, seq)
        elif m == 1:
            y = rwkv_layer(h, rwkv_mu[s], rwkv_w_rkv[s], rwkv_w0[s], rwkv_w1[s], rwkv_w2[s],
                           rwkv_a0[s], rwkv_a1[s], rwkv_a2[s], rwkv_g1[s], rwkv_g2[s],
                           rwkv_k_k[s], rwkv_k_a[s], rwkv_r_k[s], rwkv_gn_g[s], rwkv_gn_b[s],
                           rwkv_w_out[s], batch, seq)
        else:
            y = lru_layer(hb, lru_w_in[s], lru_conv_w[s], lru_conv_b[s], lru_w_gx[s],
                          lru_b_gx[s], lru_w_ga[s], lru_b_ga[s], lru_lam[s], lru_w_out[s],
                          batch, seq)
        h, hb = ln_residual(h, y, ln_g[i, 0], ln_b[i, 0])
        f = i // 2
        if i % 2 == 0:
            y = dense_ffn(hb, ffn_w_gu[f], ffn_w_down[f])
            h, hb = ln_residual(h, y, ln_g[i, 1], ln_b[i, 1])
        else:
            y2, ri = moe_ffn(h, moe_router_w[f], moe_router_b[f], moe_w_gu[f], moe_w_down[f])
            h, hb = ln_residual_moe(h, y2, ri, ln_g[i, 1], ln_b[i, 1])
    return h.reshape(batch, seq, d)
```

```python
import functools
import math

import jax
import jax.numpy as jnp
import numpy as np
from jax import lax
from jax.experimental import pallas as pl
from jax.experimental.pallas import tpu as pltpu

F32 = jnp.float32
BF16 = jnp.bfloat16

D_MODEL = 4096
DEPTH = 4
DN_ALPHA = (2 * DEPTH) ** 0.25
LN_EPS = 1e-5
CONV_K = 4
GDN_DK = 128
GDN_HEADS = D_MODEL // GDN_DK
RWKV_HEAD = 64
RWKV_GN_EPS = 64e-5
LRU_BLOCK = 256
LRU_BLOCKS = D_MODEL // LRU_BLOCK
LRU_C = 8.0
N_EXPERTS = 8

V7X_VMEM_BYTES = 64 * 1024 * 1024
VMEM_LIMIT = V7X_VMEM_BYTES - 8 * 1024 * 1024
LANES = 128
SUBLANES = 8

GDN_CHUNK = 256
RWKV_CHUNK = 64
LRU_TBLOCK = 256
INV_BASE = 16
GDN_HEADS_PER_STEP = 4
RWKV_PAIRS_PER_STEP = 16


def _params(sem):
    return pltpu.CompilerParams(dimension_semantics=sem, vmem_limit_bytes=VMEM_LIMIT)


def _dot(a, b):
    return jnp.dot(a.astype(BF16), b.astype(BF16), preferred_element_type=F32)


def _dot_nt(a, b):
    return lax.dot_general(a.astype(BF16), b.astype(BF16), (((1,), (1,)), ((), ())),
                           preferred_element_type=F32)


def _dot_tn(a, b):
    return lax.dot_general(a.astype(BF16), b.astype(BF16), (((0,), (0,)), ((), ())),
                           preferred_element_type=F32)


def _sigmoid(x):
    return 1.0 / (1.0 + jnp.exp(-x))


def _silu(x):
    return x * _sigmoid(x)


def _softplus(x):
    return jnp.maximum(x, 0.0) + jnp.log1p(jnp.exp(-jnp.abs(x)))


def _pick(n, pref):
    t = min(n, pref)
    while n % t:
        t //= 2
    return t


def _mm_kernel(x_ref, w_ref, o_ref, *, nk):
    acc = jnp.dot(x_ref[...], w_ref[...], preferred_element_type=F32)
    if nk == 1:
        o_ref[...] = acc.astype(o_ref.dtype)
    else:
        k = pl.program_id(2)

        @pl.when(k == 0)
        def _():
            o_ref[...] = acc

        @pl.when(k > 0)
        def _():
            o_ref[...] += acc


def matmul(x, w, out_dtype, xi=None, wi=None, n=None, bm=1024, bn=1024, bk=4096):
    m, kdim = x.shape[-2:]
    n = w.shape[-1] if n is None else n
    bm, bn, bk = _pick(m, bm), _pick(n, bn), _pick(kdim, bk)
    nk = kdim // bk
    if nk > 1:
        assert out_dtype == F32
    if xi is None:
        x_spec = pl.BlockSpec((bm, bk), lambda i, j, k: (i, k))
    else:
        x_spec = pl.BlockSpec((None, bm, bk), lambda i, j, k: (xi, i, k))
    if wi is None:
        w_spec = pl.BlockSpec((bk, bn), lambda i, j, k: (k, j))
    else:
        w_spec = pl.BlockSpec((None, bk, bn), lambda i, j, k: (wi, k, j))
    return pl.pallas_call(
        functools.partial(_mm_kernel, nk=nk),
        out_shape=jax.ShapeDtypeStruct((m, n), out_dtype),
        grid=(m // bm, n // bn, nk),
        in_specs=[x_spec, w_spec],
        out_specs=pl.BlockSpec((bm, bn), lambda i, j, k: (i, j)),
        compiler_params=_params(("parallel", "parallel", "arbitrary")),
        name="matmul",
    )(x, w)


def _mm_w32_kernel(x_ref, w_ref, o_ref, wb_ref):
    @pl.when(pl.program_id(1) == 0)
    def _():
        wb_ref[...] = w_ref[...].astype(BF16)

    o_ref[...] = jnp.dot(x_ref[...], wb_ref[...],
                         preferred_element_type=F32).astype(o_ref.dtype)


def matmul_w32(x, w, out_dtype, xi=None, wi=None, n=None, bm=512, bn=1024):
    m, kdim = x.shape[-2:]
    n = w.shape[-1] if n is None else n
    bm, bn = _pick(m, bm), _pick(n, bn)
    if xi is None:
        x_spec = pl.BlockSpec((bm, kdim), lambda j, i: (i, 0))
    else:
        x_spec = pl.BlockSpec((None, bm, kdim), lambda j, i: (xi, i, 0))
    if wi is None:
        w_spec = pl.BlockSpec((kdim, bn), lambda j, i: (0, j))
    else:
        w_spec = pl.BlockSpec((None, kdim, bn), lambda j, i: (wi, 0, j))
    return pl.pallas_call(
        _mm_w32_kernel,
        out_shape=jax.ShapeDtypeStruct((m, n), out_dtype),
        grid=(n // bn, m // bm),
        in_specs=[x_spec, w_spec],
        out_specs=pl.BlockSpec((bm, bn), lambda j, i: (i, j)),
        scratch_shapes=[pltpu.VMEM((kdim, bn), BF16)],
        compiler_params=_params(("parallel", "arbitrary")),
        name="matmul_w32",
    )(x, w)


def _swiglu_kernel(x_ref, wg_ref, wu_ref, o_ref, wgb_ref, wub_ref):
    @pl.when(pl.program_id(1) == 0)
    def _():
        wgb_ref[...] = wg_ref[...].astype(BF16)
        wub_ref[...] = wu_ref[...].astype(BF16)

    x = x_ref[...]
    g = jnp.dot(x, wgb_ref[...], preferred_element_type=F32)
    u = jnp.dot(x, wub_ref[...], preferred_element_type=F32)
    o_ref[...] = (_silu(g) * u).astype(o_ref.dtype)


def swiglu_up(x, w_gu, wi, bm=512, bn=512):
    m, d = x.shape
    f = w_gu.shape[2] // 2
    bm, bn = _pick(m, bm), _pick(f, bn)
    nj = f // bn
    return pl.pallas_call(
        _swiglu_kernel,
        out_shape=jax.ShapeDtypeStruct((m, f), BF16),
        grid=(nj, m // bm),
        in_specs=[pl.BlockSpec((bm, d), lambda j, i: (i, 0)),
                  pl.BlockSpec((None, d, bn), lambda j, i: (wi, 0, j)),
                  pl.BlockSpec((None, d, bn), lambda j, i: (wi, 0, j + nj))],
        out_specs=pl.BlockSpec((bm, bn), lambda j, i: (i, j)),
        scratch_shapes=[pltpu.VMEM((d, bn), BF16), pltpu.VMEM((d, bn), BF16)],
        compiler_params=_params(("parallel", "arbitrary")),
        name="swiglu_up",
    )(x, w_gu, w_gu)


def _moe_up_kernel(x_ref, wg_ref, wu_ref, comb_ref, o_ref, wgb_ref, wub_ref):
    e = pl.program_id(0)

    @pl.when(pl.program_id(2) == 0)
    def _():
        wgb_ref[...] = wg_ref[...].astype(BF16)
        wub_ref[...] = wu_ref[...].astype(BF16)

    x = x_ref[...]
    g = jnp.dot(x, wgb_ref[...], preferred_element_type=F32)
    u = jnp.dot(x, wub_ref[...], preferred_element_type=F32)
    comb = comb_ref[...]
    lane = lax.broadcasted_iota(jnp.int32, comb.shape, 1)
    ce = jnp.sum(jnp.where(lane == e, comb, 0.0), axis=1, keepdims=True)
    o_ref[...] = (_silu(g) * u * ce).astype(o_ref.dtype)


def moe_up(x, w_gu, e0, comb, bm=512, bn=512):
    m, d = x.shape
    ne = N_EXPERTS
    f = w_gu.shape[2] // 2
    bm, bn = _pick(m, bm), _pick(f, bn)
    nj = f // bn
    return pl.pallas_call(
        _moe_up_kernel,
        out_shape=jax.ShapeDtypeStruct((m, ne * f), BF16),
        grid=(ne, nj, m // bm),
        in_specs=[pl.BlockSpec((bm, d), lambda e, j, i: (i, 0)),
                  pl.BlockSpec((None, d, bn), lambda e, j, i: (e0 + e, 0, j)),
                  pl.BlockSpec((None, d, bn), lambda e, j, i: (e0 + e, 0, j + nj)),
                  pl.BlockSpec((bm, LANES), lambda e, j, i: (i, 0))],
        out_specs=pl.BlockSpec((bm, bn), lambda e, j, i: (i, e * nj + j)),
        scratch_shapes=[pltpu.VMEM((d, bn), BF16), pltpu.VMEM((d, bn), BF16)],
        compiler_params=_params(("parallel", "parallel", "arbitrary")),
        name="moe_up",
    )(x, w_gu, w_gu, comb)


def _router_kernel(x_ref, w_ref, b_ref, o_ref):
    logits = jnp.dot(x_ref[...], w_ref[...], preferred_element_type=F32,
                     precision=lax.Precision.HIGHEST) + b_ref[...]
    lane = lax.broadcasted_iota(jnp.int32, logits.shape, 1)
    neg = jnp.float32(-jnp.inf)
    logits = jnp.where(lane < N_EXPERTS, logits, neg)
    m1 = jnp.max(logits, axis=1, keepdims=True)
    i1 = jnp.min(jnp.where(logits == m1, lane, LANES), axis=1, keepdims=True)
    sel1 = lane == i1
    rest = jnp.where(sel1, neg, logits)
    m2 = jnp.max(rest, axis=1, keepdims=True)
    i2 = jnp.min(jnp.where(rest == m2, lane, LANES), axis=1, keepdims=True)
    sel2 = lane == i2
    e2 = jnp.exp(m2 - m1)
    p1 = 1.0 / (1.0 + e2)
    p2 = e2 / (1.0 + e2)
    o_ref[...] = jnp.where(sel1, p1, 0.0) + jnp.where(sel2, p2, 0.0)


def router(x, rw, rb, bm=512):
    m, d = x.shape
    bm = _pick(m, bm)
    rw_p = jnp.zeros((d, LANES), F32).at[:, :N_EXPERTS].set(rw.astype(F32))
    rb_p = jnp.zeros((1, LANES), F32).at[0, :N_EXPERTS].set(rb.astype(F32))
    return pl.pallas_call(
        _router_kernel,
        out_shape=jax.ShapeDtypeStruct((m, LANES), F32),
        grid=(m // bm,),
        in_specs=[pl.BlockSpec((bm, d), lambda i: (i, 0)),
                  pl.BlockSpec((d, LANES), lambda i: (0, 0)),
                  pl.BlockSpec((1, LANES), lambda i: (0, 0))],
        out_specs=pl.BlockSpec((bm, LANES), lambda i: (i, 0)),
        compiler_params=_params(("parallel",)),
        name="router",
    )(x, rw_p, rb_p)


def _ln_kernel(h_ref, y_ref, g_ref, b_ref, o_ref, ob_ref):
    x = DN_ALPHA * h_ref[...] + y_ref[...]
    mu = jnp.mean(x, axis=-1, keepdims=True)
    xc = x - mu
    var = jnp.mean(xc * xc, axis=-1, keepdims=True)
    o = xc * lax.rsqrt(var + LN_EPS) * g_ref[...] + b_ref[...]
    o_ref[...] = o
    ob_ref[...] = o.astype(BF16)


def ln_residual(h, y, g, b, tm=256):
    m, d = h.shape
    tm = _pick(m, tm)
    row = pl.BlockSpec((tm, d), lambda i: (i, 0))
    vec = pl.BlockSpec((1, d), lambda i: (0, 0))
    return pl.pallas_call(
        _ln_kernel,
        out_shape=(jax.ShapeDtypeStruct((m, d), F32), jax.ShapeDtypeStruct((m, d), BF16)),
        grid=(m // tm,),
        in_specs=[row, row, vec, vec],
        out_specs=(row, row),
        compiler_params=_params(("parallel",)),
        name="ln_residual",
    )(h, y, g.reshape(1, d), b.reshape(1, d))


def _causal_conv(x, tail, w):
    c = x.shape[0]
    xs = jnp.concatenate([tail, x], axis=0)
    y = xs[SUBLANES:] * w[CONV_K - 1:CONV_K]
    for j in range(CONV_K - 1):
        y = y + pltpu.roll(xs, CONV_K - 1 - j, axis=0)[SUBLANES:SUBLANES + c] * w[j:j + 1]
    return y


def _tri_masks(n, period, top):
    i = np.arange(n)[:, None]
    j = np.arange(n)[None, :]
    ms = [i == j, (i // INV_BASE) == (j // INV_BASE)]
    size = INV_BASE
    while size < top:
        bi, bj = i // size, j // size
        ms.append((bi // 2 == bj // 2) & (bi % 2 == 1) & (bj % 2 == 0))
        size *= 2
    ti, tj = i % period, j % period
    ms += [ti >= tj, ti > tj, ti <= tj]
    return jnp.asarray(np.stack(ms).astype(np.float32))


def _n_merge_levels(top):
    return (top // INV_BASE).bit_length() - 1


def _unit_lower_inverse(a_strict, m_ref, top):
    ps = [-a * m_ref[1] for a in a_strict]
    xs = [m_ref[0] + p for p in ps]
    k = 2
    while k < INV_BASE:
        ps = [_dot(p, p) for p in ps]
        xs = [x + _dot(x, p) for x, p in zip(xs, ps)]
        k *= 2
    for lvl in range(_n_merge_levels(top)):
        ys = [_dot(a * m_ref[2 + lvl], x) for a, x in zip(a_strict, xs)]
        xs = [x - _dot(x, y) for x, y in zip(xs, ys)]
    return xs


def _gdn_kernel(alog_ref, dtb_ref, q_ref, k_ref, v_ref, z_ref, ba_ref,
                cwq_ref, cwk_ref, cwv_ref, nw_ref, m_ref, o_ref, s_ref, tail_ref, *, heads):
    hg = pl.program_id(1)
    t = pl.program_id(2)
    c = q_ref.shape[0]
    nl = _n_merge_levels(c)
    causal, strict, upper = m_ref[2 + nl], m_ref[3 + nl], m_ref[4 + nl]
    eye = m_ref[0]

    @pl.when(t == 0)
    def _():
        s_ref[...] = jnp.zeros_like(s_ref)
        tail_ref[...] = jnp.zeros_like(tail_ref)

    ba = ba_ref[...]
    lane = lax.broadcasted_iota(jnp.int32, ba.shape, 1)

    hs = range(heads)
    sls = [slice(gi * GDN_DK, (gi + 1) * GDN_DK) for gi in hs]

    def conv_silu(x_ref, cw_ref, slot, sl):
        x = x_ref[:, sl].astype(F32)
        y = _causal_conv(x, tail_ref[slot, :, sl], cw_ref[:, sl])
        tail_ref[slot, :, sl] = x[c - SUBLANES:]
        return _silu(y)

    qs = [conv_silu(q_ref, cwq_ref, 0, sl) for sl in sls]
    ks = [conv_silu(k_ref, cwk_ref, 1, sl) for sl in sls]
    vs = [conv_silu(v_ref, cwv_ref, 2, sl) for sl in sls]
    qs = [q * lax.rsqrt(jnp.sum(q * q, axis=-1, keepdims=True) + 1e-6) * (GDN_DK ** -0.5)
          for q in qs]
    ks = [k * lax.rsqrt(jnp.sum(k * k, axis=-1, keepdims=True) + 1e-6) for k in ks]

    betas, g_cols, gammas = [], [], []
    for gi in hs:
        h = hg * heads + gi
        b_col = jnp.sum(jnp.where(lane == h, ba, 0.0), axis=1, keepdims=True)
        a_col = jnp.sum(jnp.where(lane == h + GDN_HEADS, ba, 0.0), axis=1, keepdims=True)
        betas.append(_sigmoid(b_col))
        g = -jnp.exp(alog_ref[h]) * _softplus(a_col + dtb_ref[h])
        g_row = jnp.sum(g * upper, axis=0, keepdims=True)
        g_col = jnp.sum(g_row * eye, axis=1, keepdims=True)
        g_cols.append(g_col)
        gammas.append(jnp.exp((g_col - g_row) * causal) * causal)

    kbs = [k * b for k, b in zip(ks, betas)]
    a_mats = [_dot_nt(kb, k) * (gm * strict) for kb, k, gm in zip(kbs, ks, gammas)]
    qks = [_dot_nt(q, k) * gm for q, k, gm in zip(qs, ks, gammas)]
    t_invs = _unit_lower_inverse(a_mats, m_ref, c)

    egs = [jnp.exp(gc) for gc in g_cols]
    uws = [_dot(ti, jnp.concatenate([v * b, kb * eg], axis=1))
           for ti, v, b, kb, eg in zip(t_invs, vs, betas, kbs, egs)]
    ss = [s_ref[gi] for gi in hs]
    ws_qs = [_dot(jnp.concatenate([uw[:, GDN_DK:], q * eg], axis=0), s)
             for uw, q, eg, s in zip(uws, qs, egs, ss)]
    v_news = [uw[:, :GDN_DK] - wq[:c] for uw, wq in zip(uws, ws_qs)]
    os_ = [wq[c:] + _dot(qk, vn) for wq, qk, vn in zip(ws_qs, qks, v_news)]
    for gi in hs:
        g_last = g_cols[gi][c - 1:c, :]
        s_ref[gi] = (ss[gi] * jnp.exp(g_last)
                     + _dot_tn(ks[gi] * jnp.exp(g_last - g_cols[gi]), v_news[gi]))
    for gi in hs:
        o = os_[gi]
        o = o * lax.rsqrt(jnp.mean(o * o, axis=-1, keepdims=True) + 1e-6) * nw_ref[...]
        o_ref[:, sls[gi]] = (o * _silu(z_ref[:, sls[gi]].astype(F32))).astype(o_ref.dtype)


def gdn_core(qkvz, ba, conv_w, a_log, dt_bias, norm_w, batch, seq):
    m = qkvz.shape[0]
    c = _pick(seq, GDN_CHUNK)
    nt = seq // c
    nh = GDN_HEADS
    g = GDN_HEADS_PER_STEP
    ng = nh // g
    w = g * GDN_DK
    masks = _tri_masks(c, c, c)

    def col(off):
        return pl.BlockSpec((c, w), lambda b, h, t: (b * nt + t, off * ng + h))

    def cw(off):
        return pl.BlockSpec((CONV_K, w), lambda b, h, t: (0, off * ng + h))

    smem = pl.BlockSpec(memory_space=pltpu.SMEM)
    return pl.pallas_call(
        functools.partial(_gdn_kernel, heads=g),
        out_shape=jax.ShapeDtypeStruct((m, D_MODEL), BF16),
        grid=(batch, ng, nt),
        in_specs=[smem, smem, col(0), col(1), col(2), col(3),
                  pl.BlockSpec((c, 2 * nh), lambda b, h, t: (b * nt + t, 0)),
                  cw(0), cw(1), cw(2),
                  pl.BlockSpec((1, GDN_DK), lambda b, h, t: (0, 0)),
                  pl.BlockSpec(masks.shape, lambda b, h, t: (0, 0, 0))],
        out_specs=pl.BlockSpec((c, w), lambda b, h, t: (b * nt + t, h)),
        scratch_shapes=[pltpu.VMEM((g, GDN_DK, GDN_DK), F32),
                        pltpu.VMEM((3, SUBLANES, w), F32)],
        compiler_params=_params(("parallel", "parallel", "arbitrary")),
        name="gdn_core",
    )(a_log.astype(F32), dt_bias.astype(F32), qkvz, qkvz, qkvz, qkvz, ba,
      conv_w, conv_w, conv_w, norm_w.reshape(1, GDN_DK), masks)


def gdn_layer(hb, s, w_in, conv_w, a_log, dt_bias, norm_w, w_out, batch, seq):
    d4 = 4 * D_MODEL
    qkvz = matmul_w32(hb, w_in, BF16, wi=s, n=d4)
    ba = matmul(hb, w_in[s, :, d4:].astype(BF16), F32)
    o = gdn_core(qkvz, ba, conv_w, a_log, dt_bias, norm_w, batch, seq)
    return matmul_w32(o, w_out, F32, wi=s)


def _rwkv_mix_kernel(x_ref, halo_ref, mu_ref, o_ref, *, seq):
    i = pl.program_id(0)
    tm = x_ref.shape[0]
    x = x_ref[...]
    prev = halo_ref[SUBLANES - 1:SUBLANES, :]
    prev = jnp.where((i * tm) % seq == 0, jnp.zeros_like(prev), prev)
    row = lax.broadcasted_iota(jnp.int32, x.shape, 0)
    xprev = jnp.where(row == 0, prev, pltpu.roll(x, 1, axis=0))
    xx = xprev - x
    for p in range(6):
        o_ref[p] = (x + xx * mu_ref[p:p + 1, :]).astype(o_ref.dtype)


def rwkv_mix(h, mu, seq, tm=256):
    m, d = h.shape
    tm = _pick(seq, tm)
    r8 = tm // SUBLANES
    return pl.pallas_call(
        functools.partial(_rwkv_mix_kernel, seq=seq),
        out_shape=jax.ShapeDtypeStruct((6, m, d), BF16),
        grid=(m // tm,),
        in_specs=[pl.BlockSpec((tm, d), lambda i: (i, 0)),
                  pl.BlockSpec((SUBLANES, d), lambda i: (jnp.maximum(i * r8 - 1, 0), 0)),
                  pl.BlockSpec((6, d), lambda i: (0, 0))],
        out_specs=pl.BlockSpec((6, tm, d), lambda i: (0, i, 0)),
        compiler_params=_params(("parallel",)),
        name="rwkv_mix",
    )(h, h, mu)


def _lora_kernel(x_ref, a_ref, b_ref, bias_ref, o_ref, *, act, epi):
    t = jnp.dot(x_ref[...], a_ref[...], preferred_element_type=F32)
    if act == "tanh":
        t = jnp.tanh(t)
    elif act == "sigmoid":
        t = _sigmoid(t)
    y = jnp.dot(t.astype(BF16), b_ref[...], preferred_element_type=F32) + bias_ref[...]
    if epi == "logdecay":
        y = -jnp.exp(-_softplus(-y) - 0.5)
    elif epi == "sigmoid":
        y = _sigmoid(y)
    o_ref[...] = y.astype(o_ref.dtype)


def lora(x, xi, a, b, bias, act, epi, out_dtype, tm=512):
    _, m, d = x.shape
    r = a.shape[1]
    n = b.shape[1]
    tm = _pick(m, tm)
    return pl.pallas_call(
        functools.partial(_lora_kernel, act=act, epi=epi),
        out_shape=jax.ShapeDtypeStruct((m, n), out_dtype),
        grid=(m // tm,),
        in_specs=[pl.BlockSpec((None, tm, d), lambda i: (xi, i, 0)),
                  pl.BlockSpec((d, r), lambda i: (0, 0)),
                  pl.BlockSpec((r, n), lambda i: (0, 0)),
                  pl.BlockSpec((1, n), lambda i: (0, 0))],
        out_specs=pl.BlockSpec((tm, n), lambda i: (i, 0)),
        compiler_params=_params(("parallel",)),
        name="lora_" + epi,
    )(x, a.astype(BF16), b.astype(BF16), bias.reshape(1, n).astype(F32))


def _group_sum(x, lo):
    s0 = jnp.sum(jnp.where(lo, x, 0.0), axis=1, keepdims=True)
    s1 = jnp.sum(jnp.where(lo, 0.0, x), axis=1, keepdims=True)
    return jnp.where(lo, s0, s1)


def _rwkv_kernel(r_ref, k_ref, v_ref, lw_ref, a_ref, g_ref, kk_ref, ka_ref, rk_ref,
                 gng_ref, gnb_ref, m_ref, o_ref, s_ref, *, pairs):
    t = pl.program_id(2)
    c = r_ref.shape[0]
    c2 = 2 * c
    nl = _n_merge_levels(c)
    incl, strict = m_ref[2 + nl], m_ref[3 + nl]

    @pl.when(t == 0)
    def _():
        s_ref[...] = jnp.zeros_like(s_ref)

    lane = lax.broadcasted_iota(jnp.int32, (c, LANES), 1)
    row = lax.broadcasted_iota(jnp.int32, (c, LANES), 0)
    lo = lane < RWKV_HEAD

    def stack(x):
        return jnp.concatenate([jnp.where(lo, x, 0.0), jnp.where(lo, 0.0, x)], axis=0)

    ps = range(pairs)
    sls = [slice(pi * LANES, (pi + 1) * LANES) for pi in ps]
    rs = [r_ref[:, sl].astype(F32) for sl in sls]
    ks = [k_ref[:, sl].astype(F32) for sl in sls]
    vs = [v_ref[:, sl].astype(F32) for sl in sls]
    lws = [lw_ref[:, sl] for sl in sls]
    a_s = [a_ref[:, sl] for sl in sls]

    kks = [k * kk_ref[:, sl] for k, sl in zip(ks, sls)]
    kks = [kk * lax.rsqrt(_group_sum(kk * kk, lo) + 1e-6) for kk in kks]
    k2s = [k * (1.0 + (a - 1.0) * ka_ref[:, sl]) for k, a, sl in zip(ks, a_s, sls)]
    kkas = [kk * a for kk, a in zip(kks, a_s)]

    cls = lws
    sft = 1
    while sft < c:
        cls = [cl + jnp.where(row >= sft, pltpu.roll(cl, sft, axis=0), 0.0) for cl in cls]
        sft *= 2

    krs = [jnp.concatenate([stack(jnp.exp(cl - lw) * kk), stack(jnp.exp(cl) * r)], axis=0)
           for cl, lw, kk, r in zip(cls, lws, kks, rs)]
    aks = [jnp.concatenate([stack(jnp.exp(-cl) * kka), stack(jnp.exp(-cl) * k2)], axis=0)
           for cl, kka, k2 in zip(cls, kkas, k2s)]
    vss = [stack(v) for v in vs]
    scs = [_dot_nt(kr, ak) for kr, ak in zip(krs, aks)]
    t_invs = _unit_lower_inverse([sc[:c2, :c2] * strict for sc in scs], m_ref, c)

    ss = [s_ref[pi] for pi in ps]
    kz_rzs = [_dot_nt(kr, s) for kr, s in zip(krs, ss)]
    avs = [_dot(sc[:c2, c2:] * strict, v) for sc, v in zip(scs, vss)]
    us = [_dot(ti, -kz[:c2] - av) for ti, kz, av in zip(t_invs, kz_rzs, avs)]
    uvs = [jnp.concatenate([u, v], axis=0) for u, v in zip(us, vss)]
    yss = [kz[c2:] + _dot(sc[c2:, :] * jnp.concatenate([incl, incl], axis=1), uv)
           for kz, sc, uv in zip(kz_rzs, scs, uvs)]
    for pi in ps:
        cl_last = cls[pi][c - 1:c, :]
        dec = jnp.exp(cl_last - cls[pi])
        ak2 = jnp.concatenate([stack(dec * kkas[pi]), stack(dec * k2s[pi])], axis=0)
        s_ref[pi] = ss[pi] * jnp.exp(cl_last) + _dot_tn(uvs[pi], ak2)

    inv_n = 1.0 / RWKV_HEAD
    for pi in ps:
        sl = sls[pi]
        y = yss[pi][:c] + yss[pi][c:]
        mean = _group_sum(y, lo) * inv_n
        yc = y - mean
        var = _group_sum(yc * yc, lo) * inv_n
        yn = yc * lax.rsqrt(var + RWKV_GN_EPS) * gng_ref[:, sl] + gnb_ref[:, sl]
        bonus = _group_sum(rs[pi] * k2s[pi] * rk_ref[:, sl], lo) * vs[pi]
        o_ref[:, sl] = ((yn + bonus) * g_ref[:, sl].astype(F32)).astype(o_ref.dtype)


def rwkv_core(r, k, v, lw, a, g, k_k, k_a, r_k, gn_g, gn_b, batch, seq):
    m, d = r.shape
    c = _pick(seq, RWKV_CHUNK)
    nt = seq // c
    pairs = RWKV_PAIRS_PER_STEP
    w = pairs * LANES
    masks = _tri_masks(2 * c, c, c)
    tok = pl.BlockSpec((c, w), lambda b, p, t: (b * nt + t, p))
    vec = pl.BlockSpec((1, w), lambda b, p, t: (0, p))
    row = lambda x: x.reshape(1, d).astype(F32)
    return pl.pallas_call(
        functools.partial(_rwkv_kernel, pairs=pairs),
        out_shape=jax.ShapeDtypeStruct((m, d), BF16),
        grid=(batch, d // w, nt),
        in_specs=[tok] * 6 + [vec] * 5 + [pl.BlockSpec(masks.shape, lambda b, p, t: (0, 0, 0))],
        out_specs=tok,
        scratch_shapes=[pltpu.VMEM((pairs, LANES, LANES), F32)],
        compiler_params=_params(("parallel", "parallel", "arbitrary")),
        name="rwkv_core",
    )(r, k, v, lw, a, g, row(k_k), row(k_a), row(r_k), row(gn_g), row(gn_b), masks)


def rwkv_layer(h, s, mu, w_rkv, w0, w1, w2, a0, a1, a2, g1, g2, k_k, k_a, r_k, gn_g, gn_b,
               w_out, batch, seq):
    xm = rwkv_mix(h, mu, seq)
    w_rkv = w_rkv.reshape((-1,) + w_rkv.shape[2:])
    r = matmul_w32(xm, w_rkv, BF16, xi=0, wi=3 * s)
    k = matmul_w32(xm, w_rkv, BF16, xi=1, wi=3 * s + 1)
    v = matmul_w32(xm, w_rkv, BF16, xi=2, wi=3 * s + 2)
    lw = lora(xm, 3, w1, w2, w0, "tanh", "logdecay", F32)
    a = lora(xm, 4, a1, a2, a0, "none", "sigmoid", F32)
    g = lora(xm, 5, g1, g2, jnp.zeros_like(a0), "sigmoid", "none", BF16)
    y = rwkv_core(r, k, v, lw, a, g, k_k, k_a, r_k, gn_g, gn_b, batch, seq)
    return matmul_w32(y, w_out, F32, wi=s)


def _lru_kernel(gate_ref, rec_ref, cw_ref, cb_ref, wgx_ref, bgx_ref, wga_ref, bga_ref,
                lam_ref, o_ref, tail_ref, h_ref):
    t = pl.program_id(2)
    c = rec_ref.shape[0]

    @pl.when(t == 0)
    def _():
        tail_ref[...] = jnp.zeros_like(tail_ref)
        h_ref[...] = jnp.zeros_like(h_ref)

    x = rec_ref[...].astype(F32)
    u = _causal_conv(x, tail_ref[...], cw_ref[...]) + cb_ref[...]
    tail_ref[...] = x[c - SUBLANES:]
    ub = u.astype(BF16)
    i_t = _sigmoid(jnp.dot(ub, wgx_ref[...], preferred_element_type=F32) + bgx_ref[...])
    r_t = _sigmoid(jnp.dot(ub, wga_ref[...], preferred_element_type=F32) + bga_ref[...])
    log_a = -LRU_C * r_t * _softplus(-lam_ref[...])
    a = jnp.exp(log_a)
    th = jnp.tanh(log_a)
    b = jnp.sqrt(-2.0 * th / (1.0 - th)) * (i_t * u)

    row = lax.broadcasted_iota(jnp.int32, a.shape, 0)
    sft = 1
    while sft < c:
        keep = row >= sft
        a_s = jnp.where(keep, pltpu.roll(a, sft, axis=0), 1.0)
        b_s = jnp.where(keep, pltpu.roll(b, sft, axis=0), 0.0)
        b = a * b_s + b
        a = a * a_s
        sft *= 2
    hcur = b + a * h_ref[0:1, :]
    h_ref[0:1, :] = hcur[c - 1:c, :]

    gt = gate_ref[...].astype(F32)
    gelu = 0.5 * gt * (1.0 + jnp.tanh(math.sqrt(2.0 / math.pi) * (gt + 0.044715 * gt * gt * gt)))
    o_ref[...] = (hcur * gelu).astype(o_ref.dtype)


def lru_core(proj, conv_w, conv_b, w_gx, b_gx, w_ga, b_ga, lam, batch, seq):
    m = proj.shape[0]
    c = _pick(seq, LRU_TBLOCK)
    nt = seq // c
    nb = LRU_BLOCKS
    w = LRU_BLOCK
    vec = pl.BlockSpec((1, w), lambda b, j, t: (0, j))
    blk = pl.BlockSpec((None, w, w), lambda b, j, t: (j, 0, 0))
    row = lambda x: x.reshape(1, nb * w).astype(F32)
    return pl.pallas_call(
        _lru_kernel,
        out_shape=jax.ShapeDtypeStruct((m, nb * w), BF16),
        grid=(batch, nb, nt),
        in_specs=[pl.BlockSpec((c, w), lambda b, j, t: (b * nt + t, j)),
                  pl.BlockSpec((c, w), lambda b, j, t: (b * nt + t, nb + j)),
                  pl.BlockSpec((CONV_K, w), lambda b, j, t: (0, j)),
                  vec, blk, vec, blk, vec, vec],
        out_specs=pl.BlockSpec((c, w), lambda b, j, t: (b * nt + t, j)),
        scratch_shapes=[pltpu.VMEM((SUBLANES, w), F32), pltpu.VMEM((SUBLANES, w), F32)],
        compiler_params=_params(("parallel", "parallel", "arbitrary")),
        name="lru_core",
    )(proj, proj, conv_w, row(conv_b), w_gx.astype(BF16), row(b_gx), w_ga.astype(BF16),
      row(b_ga), row(lam))


def lru_layer(hb, s, w_in, conv_w, conv_b, w_gx, b_gx, w_ga, b_ga, lam, w_out, batch, seq):
    proj = matmul_w32(hb, w_in, BF16, wi=s)
    y = lru_core(proj, conv_w, conv_b, w_gx, b_gx, w_ga, b_ga, lam, batch, seq)
    return matmul_w32(y, w_out, F32, wi=s)


def dense_ffn(hb, f, w_gu, w_down):
    act = swiglu_up(hb, w_gu, f)
    return matmul(act, w_down.astype(BF16), F32)


def moe_ffn(h, hb, f, router_w, router_b, w_gu, w_down):
    comb = router(h, router_w, router_b)
    act = moe_up(hb, w_gu.reshape((-1,) + w_gu.shape[2:]), f * N_EXPERTS, comb)
    ne, fe, d = w_down.shape
    return matmul(act, w_down.astype(BF16).reshape(ne * fe, d), F32)


def kernel(x, ln_g, ln_b, gdn_w_in, gdn_conv_w, gdn_a_log, gdn_dt_bias, gdn_norm_w, gdn_w_out, rwkv_mu, rwkv_w_rkv, rwkv_w0, rwkv_w1, rwkv_w2, rwkv_a0, rwkv_a1, rwkv_a2, rwkv_g1, rwkv_g2, rwkv_k_k, rwkv_k_a, rwkv_r_k, rwkv_gn_g, rwkv_gn_b, rwkv_w_out, lru_w_in, lru_conv_w, lru_conv_b, lru_w_gx, lru_b_gx, lru_w_ga, lru_b_ga, lru_lam, lru_w_out, ffn_w_gu, ffn_w_down, moe_router_w, moe_router_b, moe_w_gu, moe_w_down):
    batch, seq, d = x.shape
    h = x.reshape(batch * seq, d)
    hb = h.astype(BF16)
    for i in range(DEPTH):
        m, s = i % 3, i // 3
        if m == 0:
            y = gdn_layer(hb, s, gdn_w_in, gdn_conv_w[s], gdn_a_log[s], gdn_dt_bias[s],
                          gdn_norm_w[s], gdn_w_out, batch, seq)
        elif m == 1:
            y = rwkv_layer(h, s, rwkv_mu[s], rwkv_w_rkv, rwkv_w0[s], rwkv_w1[s], rwkv_w2[s],
                           rwkv_a0[s], rwkv_a1[s], rwkv_a2[s], rwkv_g1[s], rwkv_g2[s],
                           rwkv_k_k[s], rwkv_k_a[s], rwkv_r_k[s], rwkv_gn_g[s], rwkv_gn_b[s],
                           rwkv_w_out, batch, seq)
        else:
            y = lru_layer(hb, s, lru_w_in, lru_conv_w[s], lru_conv_b[s], lru_w_gx[s],
                          lru_b_gx[s], lru_w_ga[s], lru_b_ga[s], lru_lam[s], lru_w_out,
                          batch, seq)
        h, hb = ln_residual(h, y, ln_g[i, 0], ln_b[i, 0])
        f = i // 2
        if i % 2 == 0:
            y = dense_ffn(hb, f, ffn_w_gu, ffn_w_down[f])
        else:
            y = moe_ffn(h, hb, f, moe_router_w[f], moe_router_b[f], moe_w_gu, moe_w_down[f])
        h, hb = ln_residual(h, y, ln_g[i, 1], ln_b[i, 1])
    return h.reshape(batch, seq, d)
```

```python
import functools
import math

import jax
import jax.numpy as jnp
import numpy as np
from jax import lax
from jax.experimental import pallas as pl
from jax.experimental.pallas import tpu as pltpu

F32 = jnp.float32
BF16 = jnp.bfloat16

D_MODEL = 4096
DEPTH = 4
DN_ALPHA = (2 * DEPTH) ** 0.25
LN_EPS = 1e-5
CONV_K = 4
GDN_DK = 128
GDN_HEADS = D_MODEL // GDN_DK
RWKV_HEAD = 64
RWKV_GN_EPS = 64e-5
LRU_BLOCK = 256
LRU_BLOCKS = D_MODEL // LRU_BLOCK
LRU_C = 8.0
N_EXPERTS = 8

V7X_VMEM_BYTES = 64 * 1024 * 1024
VMEM_LIMIT = V7X_VMEM_BYTES - 8 * 1024 * 1024
LANES = 128
SUBLANES = 8

GDN_CHUNK = 256
RWKV_CHUNK = 64
LRU_TBLOCK = 256
INV_BASE = 16
GDN_HEADS_PER_STEP = 4
RWKV_PAIRS_PER_STEP = 16


def _params(sem):
    return pltpu.CompilerParams(dimension_semantics=sem, vmem_limit_bytes=VMEM_LIMIT)


def _dot(a, b):
    return jnp.dot(a.astype(BF16), b.astype(BF16), preferred_element_type=F32)


def _dot_nt(a, b):
    return lax.dot_general(a.astype(BF16), b.astype(BF16), (((1,), (1,)), ((), ())),
                           preferred_element_type=F32)


def _dot_tn(a, b):
    return lax.dot_general(a.astype(BF16), b.astype(BF16), (((0,), (0,)), ((), ())),
                           preferred_element_type=F32)


def _sigmoid(x):
    return 1.0 / (1.0 + jnp.exp(-x))


def _silu(x):
    return x * _sigmoid(x)


def _softplus(x):
    return jnp.maximum(x, 0.0) + jnp.log1p(jnp.exp(-jnp.abs(x)))


def _pick(n, pref):
    t = min(n, pref)
    while n % t:
        t //= 2
    return t


def _mm_kernel(x_ref, w_ref, o_ref, *, nk):
    acc = jnp.dot(x_ref[...], w_ref[...], preferred_element_type=F32)
    if nk == 1:
        o_ref[...] = acc.astype(o_ref.dtype)
    else:
        k = pl.program_id(2)

        @pl.when(k == 0)
        def _():
            o_ref[...] = acc

        @pl.when(k > 0)
        def _():
            o_ref[...] += acc


def matmul(x, w, out_dtype, xi=None, wi=None, n=None, bm=1024, bn=1024, bk=4096):
    m, kdim = x.shape[-2:]
    n = w.shape[-1] if n is None else n
    bm, bn, bk = _pick(m, bm), _pick(n, bn), _pick(kdim, bk)
    nk = kdim // bk
    if nk > 1:
        assert out_dtype == F32
    if xi is None:
        x_spec = pl.BlockSpec((bm, bk), lambda i, j, k: (i, k))
    else:
        x_spec = pl.BlockSpec((None, bm, bk), lambda i, j, k: (xi, i, k))
    if wi is None:
        w_spec = pl.BlockSpec((bk, bn), lambda i, j, k: (k, j))
    else:
        w_spec = pl.BlockSpec((None, bk, bn), lambda i, j, k: (wi, k, j))
    return pl.pallas_call(
        functools.partial(_mm_kernel, nk=nk),
        out_shape=jax.ShapeDtypeStruct((m, n), out_dtype),
        grid=(m // bm, n // bn, nk),
        in_specs=[x_spec, w_spec],
        out_specs=pl.BlockSpec((bm, bn), lambda i, j, k: (i, j)),
        compiler_params=_params(("parallel", "parallel", "arbitrary")),
        name="matmul",
    )(x, w)


def _mm_w32_kernel(x_ref, w_ref, o_ref, wb_ref):
    @pl.when(pl.program_id(1) == 0)
    def _():
        wb_ref[...] = w_ref[...].astype(BF16)

    o_ref[...] = jnp.dot(x_ref[...], wb_ref[...],
                         preferred_element_type=F32).astype(o_ref.dtype)


def matmul_w32(x, w, out_dtype, xi=None, wi=None, n=None, bm=512, bn=1024):
    m, kdim = x.shape[-2:]
    n = w.shape[-1] if n is None else n
    bm, bn = _pick(m, bm), _pick(n, bn)
    if xi is None:
        x_spec = pl.BlockSpec((bm, kdim), lambda j, i: (i, 0))
    else:
        x_spec = pl.BlockSpec((None, bm, kdim), lambda j, i: (xi, i, 0))
    wi = () if wi is None else (wi if isinstance(wi, tuple) else (wi,))
    w_spec = pl.BlockSpec((None,) * len(wi) + (kdim, bn), lambda j, i: wi + (0, j))
    return pl.pallas_call(
        _mm_w32_kernel,
        out_shape=jax.ShapeDtypeStruct((m, n), out_dtype),
        grid=(n // bn, m // bm),
        in_specs=[x_spec, w_spec],
        out_specs=pl.BlockSpec((bm, bn), lambda j, i: (i, j)),
        scratch_shapes=[pltpu.VMEM((kdim, bn), BF16)],
        compiler_params=_params(("parallel", "arbitrary")),
        name="matmul_w32",
    )(x, w)


def _swiglu_kernel(x_ref, wg_ref, wu_ref, o_ref, wgb_ref, wub_ref):
    @pl.when(pl.program_id(1) == 0)
    def _():
        wgb_ref[...] = wg_ref[...].astype(BF16)
        wub_ref[...] = wu_ref[...].astype(BF16)

    x = x_ref[...]
    g = jnp.dot(x, wgb_ref[...], preferred_element_type=F32)
    u = jnp.dot(x, wub_ref[...], preferred_element_type=F32)
    o_ref[...] = (_silu(g) * u).astype(o_ref.dtype)


def swiglu_up(x, w_gu, wi, bm=512, bn=512):
    m, d = x.shape
    f = w_gu.shape[2] // 2
    bm, bn = _pick(m, bm), _pick(f, bn)
    nj = f // bn
    return pl.pallas_call(
        _swiglu_kernel,
        out_shape=jax.ShapeDtypeStruct((m, f), BF16),
        grid=(nj, m // bm),
        in_specs=[pl.BlockSpec((bm, d), lambda j, i: (i, 0)),
                  pl.BlockSpec((None, d, bn), lambda j, i: (wi, 0, j)),
                  pl.BlockSpec((None, d, bn), lambda j, i: (wi, 0, j + nj))],
        out_specs=pl.BlockSpec((bm, bn), lambda j, i: (i, j)),
        scratch_shapes=[pltpu.VMEM((d, bn), BF16), pltpu.VMEM((d, bn), BF16)],
        compiler_params=_params(("parallel", "arbitrary")),
        name="swiglu_up",
    )(x, w_gu, w_gu)


def _moe_up_kernel(x_ref, wg_ref, wu_ref, comb_ref, o_ref, wgb_ref, wub_ref):
    e = pl.program_id(0)

    @pl.when(pl.program_id(2) == 0)
    def _():
        wgb_ref[...] = wg_ref[...].astype(BF16)
        wub_ref[...] = wu_ref[...].astype(BF16)

    x = x_ref[...]
    g = jnp.dot(x, wgb_ref[...], preferred_element_type=F32)
    u = jnp.dot(x, wub_ref[...], preferred_element_type=F32)
    comb = comb_ref[...]
    lane = lax.broadcasted_iota(jnp.int32, comb.shape, 1)
    ce = jnp.sum(jnp.where(lane == e, comb, 0.0), axis=1, keepdims=True)
    o_ref[...] = (_silu(g) * u * ce).astype(o_ref.dtype)


def moe_up(x, w_gu, li, comb, bm=512, bn=512):
    m, d = x.shape
    ne = N_EXPERTS
    f = w_gu.shape[3] // 2
    bm, bn = _pick(m, bm), _pick(f, bn)
    nj = f // bn
    return pl.pallas_call(
        _moe_up_kernel,
        out_shape=jax.ShapeDtypeStruct((m, ne * f), BF16),
        grid=(ne, nj, m // bm),
        in_specs=[pl.BlockSpec((bm, d), lambda e, j, i: (i, 0)),
                  pl.BlockSpec((None, None, d, bn), lambda e, j, i: (li, e, 0, j)),
                  pl.BlockSpec((None, None, d, bn), lambda e, j, i: (li, e, 0, j + nj)),
                  pl.BlockSpec((bm, LANES), lambda e, j, i: (i, 0))],
        out_specs=pl.BlockSpec((bm, bn), lambda e, j, i: (i, e * nj + j)),
        scratch_shapes=[pltpu.VMEM((d, bn), BF16), pltpu.VMEM((d, bn), BF16)],
        compiler_params=_params(("parallel", "parallel", "arbitrary")),
        name="moe_up",
    )(x, w_gu, w_gu, comb)


def _router_kernel(x_ref, w_ref, b_ref, o_ref):
    logits = jnp.dot(x_ref[...], w_ref[...], preferred_element_type=F32,
                     precision=lax.Precision.HIGHEST) + b_ref[...]
    lane = lax.broadcasted_iota(jnp.int32, logits.shape, 1)
    neg = jnp.float32(-jnp.inf)
    logits = jnp.where(lane < N_EXPERTS, logits, neg)
    m1 = jnp.max(logits, axis=1, keepdims=True)
    i1 = jnp.min(jnp.where(logits == m1, lane, LANES), axis=1, keepdims=True)
    sel1 = lane == i1
    rest = jnp.where(sel1, neg, logits)
    m2 = jnp.max(rest, axis=1, keepdims=True)
    i2 = jnp.min(jnp.where(rest == m2, lane, LANES), axis=1, keepdims=True)
    sel2 = lane == i2
    e2 = jnp.exp(m2 - m1)
    p1 = 1.0 / (1.0 + e2)
    p2 = e2 / (1.0 + e2)
    o_ref[...] = jnp.where(sel1, p1, 0.0) + jnp.where(sel2, p2, 0.0)


def router(x, rw, rb, bm=512):
    m, d = x.shape
    bm = _pick(m, bm)
    rw_p = jnp.zeros((d, LANES), F32).at[:, :N_EXPERTS].set(rw.astype(F32))
    rb_p = jnp.zeros((1, LANES), F32).at[0, :N_EXPERTS].set(rb.astype(F32))
    return pl.pallas_call(
        _router_kernel,
        out_shape=jax.ShapeDtypeStruct((m, LANES), F32),
        grid=(m // bm,),
        in_specs=[pl.BlockSpec((bm, d), lambda i: (i, 0)),
                  pl.BlockSpec((d, LANES), lambda i: (0, 0)),
                  pl.BlockSpec((1, LANES), lambda i: (0, 0))],
        out_specs=pl.BlockSpec((bm, LANES), lambda i: (i, 0)),
        compiler_params=_params(("parallel",)),
        name="router",
    )(x, rw_p, rb_p)


def _ln_kernel(h_ref, y_ref, g_ref, b_ref, o_ref, ob_ref):
    x = DN_ALPHA * h_ref[...] + y_ref[...]
    mu = jnp.mean(x, axis=-1, keepdims=True)
    xc = x - mu
    var = jnp.mean(xc * xc, axis=-1, keepdims=True)
    o = xc * lax.rsqrt(var + LN_EPS) * g_ref[...] + b_ref[...]
    o_ref[...] = o
    ob_ref[...] = o.astype(BF16)


def ln_residual(h, y, g, b, tm=256):
    m, d = h.shape
    tm = _pick(m, tm)
    row = pl.BlockSpec((tm, d), lambda i: (i, 0))
    vec = pl.BlockSpec((1, d), lambda i: (0, 0))
    return pl.pallas_call(
        _ln_kernel,
        out_shape=(jax.ShapeDtypeStruct((m, d), F32), jax.ShapeDtypeStruct((m, d), BF16)),
        grid=(m // tm,),
        in_specs=[row, row, vec, vec],
        out_specs=(row, row),
        compiler_params=_params(("parallel",)),
        name="ln_residual",
    )(h, y, g.reshape(1, d), b.reshape(1, d))


def _causal_conv(x, tail, w):
    c = x.shape[0]
    xs = jnp.concatenate([tail, x], axis=0)
    y = xs[SUBLANES:] * w[CONV_K - 1:CONV_K]
    for j in range(CONV_K - 1):
        y = y + pltpu.roll(xs, CONV_K - 1 - j, axis=0)[SUBLANES:SUBLANES + c] * w[j:j + 1]
    return y


def _tri_masks(n, period, top):
    i = np.arange(n)[:, None]
    j = np.arange(n)[None, :]
    ms = [i == j, (i // INV_BASE) == (j // INV_BASE)]
    size = INV_BASE
    while size < top:
        bi, bj = i // size, j // size
        ms.append((bi // 2 == bj // 2) & (bi % 2 == 1) & (bj % 2 == 0))
        size *= 2
    ti, tj = i % period, j % period
    ms += [ti >= tj, ti > tj, ti <= tj]
    return jnp.asarray(np.stack(ms).astype(np.float32))


def _n_merge_levels(top):
    return (top // INV_BASE).bit_length() - 1


def _unit_lower_inverse(a_strict, m_ref, top):
    ps = [-a * m_ref[1] for a in a_strict]
    xs = [m_ref[0] + p for p in ps]
    k = 2
    while k < INV_BASE:
        ps = [_dot(p, p) for p in ps]
        xs = [x + _dot(x, p) for x, p in zip(xs, ps)]
        k *= 2
    for lvl in range(_n_merge_levels(top)):
        ys = [_dot(a * m_ref[2 + lvl], x) for a, x in zip(a_strict, xs)]
        xs = [x - _dot(x, y) for x, y in zip(xs, ys)]
    return xs


def _gdn_kernel(alog_ref, dtb_ref, q_ref, k_ref, v_ref, z_ref, ba_ref,
                cwq_ref, cwk_ref, cwv_ref, nw_ref, m_ref, o_ref, s_ref, tail_ref, *, heads):
    hg = pl.program_id(1)
    t = pl.program_id(2)
    c = q_ref.shape[0]
    nl = _n_merge_levels(c)
    causal, strict, upper = m_ref[2 + nl], m_ref[3 + nl], m_ref[4 + nl]
    eye = m_ref[0]

    @pl.when(t == 0)
    def _():
        s_ref[...] = jnp.zeros_like(s_ref)
        tail_ref[...] = jnp.zeros_like(tail_ref)

    ba = ba_ref[...]
    lane = lax.broadcasted_iota(jnp.int32, ba.shape, 1)

    hs = range(heads)
    sls = [slice(gi * GDN_DK, (gi + 1) * GDN_DK) for gi in hs]

    def conv_silu(x_ref, cw_ref, slot, sl):
        x = x_ref[:, sl].astype(F32)
        y = _causal_conv(x, tail_ref[slot, :, sl], cw_ref[:, sl])
        tail_ref[slot, :, sl] = x[c - SUBLANES:]
        return _silu(y)

    qs = [conv_silu(q_ref, cwq_ref, 0, sl) for sl in sls]
    ks = [conv_silu(k_ref, cwk_ref, 1, sl) for sl in sls]
    vs = [conv_silu(v_ref, cwv_ref, 2, sl) for sl in sls]
    qs = [q * lax.rsqrt(jnp.sum(q * q, axis=-1, keepdims=True) + 1e-6) * (GDN_DK ** -0.5)
          for q in qs]
    ks = [k * lax.rsqrt(jnp.sum(k * k, axis=-1, keepdims=True) + 1e-6) for k in ks]

    betas, g_cols, gammas = [], [], []
    for gi in hs:
        h = hg * heads + gi
        b_col = jnp.sum(jnp.where(lane == h, ba, 0.0), axis=1, keepdims=True)
        a_col = jnp.sum(jnp.where(lane == h + GDN_HEADS, ba, 0.0), axis=1, keepdims=True)
        betas.append(_sigmoid(b_col))
        g = -jnp.exp(alog_ref[h]) * _softplus(a_col + dtb_ref[h])
        g_row = jnp.sum(g * upper, axis=0, keepdims=True)
        g_col = jnp.sum(g_row * eye, axis=1, keepdims=True)
        g_cols.append(g_col)
        gammas.append(jnp.exp((g_col - g_row) * causal) * causal)

    kbs = [k * b for k, b in zip(ks, betas)]
    a_mats = [_dot_nt(kb, k) * (gm * strict) for kb, k, gm in zip(kbs, ks, gammas)]
    qks = [_dot_nt(q, k) * gm for q, k, gm in zip(qs, ks, gammas)]
    t_invs = _unit_lower_inverse(a_mats, m_ref, c)

    egs = [jnp.exp(gc) for gc in g_cols]
    uws = [_dot(ti, jnp.concatenate([v * b, kb * eg], axis=1))
           for ti, v, b, kb, eg in zip(t_invs, vs, betas, kbs, egs)]
    ss = [s_ref[gi] for gi in hs]
    ws_qs = [_dot(jnp.concatenate([uw[:, GDN_DK:], q * eg], axis=0), s)
             for uw, q, eg, s in zip(uws, qs, egs, ss)]
    v_news = [uw[:, :GDN_DK] - wq[:c] for uw, wq in zip(uws, ws_qs)]
    os_ = [wq[c:] + _dot(qk, vn) for wq, qk, vn in zip(ws_qs, qks, v_news)]
    for gi in hs:
        g_last = g_cols[gi][c - 1:c, :]
        s_ref[gi] = (ss[gi] * jnp.exp(g_last)
                     + _dot_tn(ks[gi] * jnp.exp(g_last - g_cols[gi]), v_news[gi]))
    for gi in hs:
        o = os_[gi]
        o = o * lax.rsqrt(jnp.mean(o * o, axis=-1, keepdims=True) + 1e-6) * nw_ref[...]
        o_ref[:, sls[gi]] = (o * _silu(z_ref[:, sls[gi]].astype(F32))).astype(o_ref.dtype)


def gdn_core(qkvz, ba, conv_w, a_log, dt_bias, norm_w, batch, seq):
    m = qkvz.shape[0]
    c = _pick(seq, GDN_CHUNK)
    nt = seq // c
    nh = GDN_HEADS
    g = GDN_HEADS_PER_STEP
    ng = nh // g
    w = g * GDN_DK
    masks = _tri_masks(c, c, c)

    def col(off):
        return pl.BlockSpec((c, w), lambda b, h, t: (b * nt + t, off * ng + h))

    def cw(off):
        return pl.BlockSpec((CONV_K, w), lambda b, h, t: (0, off * ng + h))

    smem = pl.BlockSpec(memory_space=pltpu.SMEM)
    return pl.pallas_call(
        functools.partial(_gdn_kernel, heads=g),
        out_shape=jax.ShapeDtypeStruct((m, D_MODEL), BF16),
        grid=(batch, ng, nt),
        in_specs=[smem, smem, col(0), col(1), col(2), col(3),
                  pl.BlockSpec((c, 2 * nh), lambda b, h, t: (b * nt + t, 0)),
                  cw(0), cw(1), cw(2),
                  pl.BlockSpec((1, GDN_DK), lambda b, h, t: (0, 0)),
                  pl.BlockSpec(masks.shape, lambda b, h, t: (0, 0, 0))],
        out_specs=pl.BlockSpec((c, w), lambda b, h, t: (b * nt + t, h)),
        scratch_shapes=[pltpu.VMEM((g, GDN_DK, GDN_DK), F32),
                        pltpu.VMEM((3, SUBLANES, w), F32)],
        compiler_params=_params(("parallel", "parallel", "arbitrary")),
        name="gdn_core",
    )(a_log.astype(F32), dt_bias.astype(F32), qkvz, qkvz, qkvz, qkvz, ba,
      conv_w, conv_w, conv_w, norm_w.reshape(1, GDN_DK), masks)


def gdn_layer(hb, s, w_in, conv_w, a_log, dt_bias, norm_w, w_out, batch, seq):
    d4 = 4 * D_MODEL
    qkvz = matmul_w32(hb, w_in, BF16, wi=s, n=d4)
    ba = matmul(hb, w_in[s, :, d4:].astype(BF16), F32)
    o = gdn_core(qkvz, ba, conv_w, a_log, dt_bias, norm_w, batch, seq)
    return matmul_w32(o, w_out, F32, wi=s)


def _rwkv_mix_kernel(x_ref, halo_ref, mu_ref, o_ref, *, seq):
    i = pl.program_id(0)
    tm = x_ref.shape[0]
    x = x_ref[...]
    prev = halo_ref[SUBLANES - 1:SUBLANES, :]
    prev = jnp.where((i * tm) % seq == 0, jnp.zeros_like(prev), prev)
    row = lax.broadcasted_iota(jnp.int32, x.shape, 0)
    xprev = jnp.where(row == 0, prev, pltpu.roll(x, 1, axis=0))
    xx = xprev - x
    for p in range(6):
        o_ref[p] = (x + xx * mu_ref[p:p + 1, :]).astype(o_ref.dtype)


def rwkv_mix(h, mu, seq, tm=256):
    m, d = h.shape
    tm = _pick(seq, tm)
    r8 = tm // SUBLANES
    return pl.pallas_call(
        functools.partial(_rwkv_mix_kernel, seq=seq),
        out_shape=jax.ShapeDtypeStruct((6, m, d), BF16),
        grid=(m // tm,),
        in_specs=[pl.BlockSpec((tm, d), lambda i: (i, 0)),
                  pl.BlockSpec((SUBLANES, d), lambda i: (jnp.maximum(i * r8 - 1, 0), 0)),
                  pl.BlockSpec((6, d), lambda i: (0, 0))],
        out_specs=pl.BlockSpec((6, tm, d), lambda i: (0, i, 0)),
        compiler_params=_params(("parallel",)),
        name="rwkv_mix",
    )(h, h, mu)


def _lora_kernel(x_ref, a_ref, b_ref, bias_ref, o_ref, *, act, epi):
    t = jnp.dot(x_ref[...], a_ref[...], preferred_element_type=F32)
    if act == "tanh":
        t = jnp.tanh(t)
    elif act == "sigmoid":
        t = _sigmoid(t)
    y = jnp.dot(t.astype(BF16), b_ref[...], preferred_element_type=F32) + bias_ref[...]
    if epi == "logdecay":
        y = -jnp.exp(-_softplus(-y) - 0.5)
    elif epi == "sigmoid":
        y = _sigmoid(y)
    o_ref[...] = y.astype(o_ref.dtype)


def lora(x, xi, a, b, bias, act, epi, out_dtype, tm=512):
    _, m, d = x.shape
    r = a.shape[1]
    n = b.shape[1]
    tm = _pick(m, tm)
    return pl.pallas_call(
        functools.partial(_lora_kernel, act=act, epi=epi),
        out_shape=jax.ShapeDtypeStruct((m, n), out_dtype),
        grid=(m // tm,),
        in_specs=[pl.BlockSpec((None, tm, d), lambda i: (xi, i, 0)),
                  pl.BlockSpec((d, r), lambda i: (0, 0)),
                  pl.BlockSpec((r, n), lambda i: (0, 0)),
                  pl.BlockSpec((1, n), lambda i: (0, 0))],
        out_specs=pl.BlockSpec((tm, n), lambda i: (i, 0)),
        compiler_params=_params(("parallel",)),
        name="lora_" + epi,
    )(x, a.astype(BF16), b.astype(BF16), bias.reshape(1, n).astype(F32))


def _group_sum(x, lo):
    s0 = jnp.sum(jnp.where(lo, x, 0.0), axis=1, keepdims=True)
    s1 = jnp.sum(jnp.where(lo, 0.0, x), axis=1, keepdims=True)
    return jnp.where(lo, s0, s1)


def _rwkv_kernel(r_ref, k_ref, v_ref, lw_ref, a_ref, g_ref, kk_ref, ka_ref, rk_ref,
                 gng_ref, gnb_ref, m_ref, o_ref, s_ref, *, pairs):
    t = pl.program_id(2)
    c = r_ref.shape[0]
    c2 = 2 * c
    nl = _n_merge_levels(c)
    incl, strict = m_ref[2 + nl], m_ref[3 + nl]

    @pl.when(t == 0)
    def _():
        s_ref[...] = jnp.zeros_like(s_ref)

    lane = lax.broadcasted_iota(jnp.int32, (c, LANES), 1)
    row = lax.broadcasted_iota(jnp.int32, (c, LANES), 0)
    lo = lane < RWKV_HEAD

    def stack(x):
        return jnp.concatenate([jnp.where(lo, x, 0.0), jnp.where(lo, 0.0, x)], axis=0)

    ps = range(pairs)
    sls = [slice(pi * LANES, (pi + 1) * LANES) for pi in ps]
    rs = [r_ref[:, sl].astype(F32) for sl in sls]
    ks = [k_ref[:, sl].astype(F32) for sl in sls]
    vs = [v_ref[:, sl].astype(F32) for sl in sls]
    lws = [lw_ref[:, sl] for sl in sls]
    a_s = [a_ref[:, sl] for sl in sls]

    kks = [k * kk_ref[:, sl] for k, sl in zip(ks, sls)]
    kks = [kk * lax.rsqrt(_group_sum(kk * kk, lo) + 1e-6) for kk in kks]
    k2s = [k * (1.0 + (a - 1.0) * ka_ref[:, sl]) for k, a, sl in zip(ks, a_s, sls)]
    kkas = [kk * a for kk, a in zip(kks, a_s)]

    cls = lws
    sft = 1
    while sft < c:
        cls = [cl + jnp.where(row >= sft, pltpu.roll(cl, sft, axis=0), 0.0) for cl in cls]
        sft *= 2

    krs = [jnp.concatenate([stack(jnp.exp(cl - lw) * kk), stack(jnp.exp(cl) * r)], axis=0)
           for cl, lw, kk, r in zip(cls, lws, kks, rs)]
    aks = [jnp.concatenate([stack(jnp.exp(-cl) * kka), stack(jnp.exp(-cl) * k2)], axis=0)
           for cl, kka, k2 in zip(cls, kkas, k2s)]
    vss = [stack(v) for v in vs]
    scs = [_dot_nt(kr, ak) for kr, ak in zip(krs, aks)]
    t_invs = _unit_lower_inverse([sc[:c2, :c2] * strict for sc in scs], m_ref, c)

    ss = [s_ref[pi] for pi in ps]
    kz_rzs = [_dot_nt(kr, s) for kr, s in zip(krs, ss)]
    avs = [_dot(sc[:c2, c2:] * strict, v) for sc, v in zip(scs, vss)]
    us = [_dot(ti, -kz[:c2] - av) for ti, kz, av in zip(t_invs, kz_rzs, avs)]
    uvs = [jnp.concatenate([u, v], axis=0) for u, v in zip(us, vss)]
    yss = [kz[c2:] + _dot(sc[c2:, :] * jnp.concatenate([incl, incl], axis=1), uv)
           for kz, sc, uv in zip(kz_rzs, scs, uvs)]
    for pi in ps:
        cl_last = cls[pi][c - 1:c, :]
        dec = jnp.exp(cl_last - cls[pi])
        ak2 = jnp.concatenate([stack(dec * kkas[pi]), stack(dec * k2s[pi])], axis=0)
        s_ref[pi] = ss[pi] * jnp.exp(cl_last) + _dot_tn(uvs[pi], ak2)

    inv_n = 1.0 / RWKV_HEAD
    for pi in ps:
        sl = sls[pi]
        y = yss[pi][:c] + yss[pi][c:]
        mean = _group_sum(y, lo) * inv_n
        yc = y - mean
        var = _group_sum(yc * yc, lo) * inv_n
        yn = yc * lax.rsqrt(var + RWKV_GN_EPS) * gng_ref[:, sl] + gnb_ref[:, sl]
        bonus = _group_sum(rs[pi] * k2s[pi] * rk_ref[:, sl], lo) * vs[pi]
        o_ref[:, sl] = ((yn + bonus) * g_ref[:, sl].astype(F32)).astype(o_ref.dtype)


def rwkv_core(r, k, v, lw, a, g, k_k, k_a, r_k, gn_g, gn_b, batch, seq):
    m, d = r.shape
    c = _pick(seq, RWKV_CHUNK)
    nt = seq // c
    pairs = RWKV_PAIRS_PER_STEP
    w = pairs * LANES
    masks = _tri_masks(2 * c, c, c)
    tok = pl.BlockSpec((c, w), lambda b, p, t: (b * nt + t, p))
    vec = pl.BlockSpec((1, w), lambda b, p, t: (0, p))
    row = lambda x: x.reshape(1, d).astype(F32)
    return pl.pallas_call(
        functools.partial(_rwkv_kernel, pairs=pairs),
        out_shape=jax.ShapeDtypeStruct((m, d), BF16),
        grid=(batch, d // w, nt),
        in_specs=[tok] * 6 + [vec] * 5 + [pl.BlockSpec(masks.shape, lambda b, p, t: (0, 0, 0))],
        out_specs=tok,
        scratch_shapes=[pltpu.VMEM((pairs, LANES, LANES), F32)],
        compiler_params=_params(("parallel", "parallel", "arbitrary")),
        name="rwkv_core",
    )(r, k, v, lw, a, g, row(k_k), row(k_a), row(r_k), row(gn_g), row(gn_b), masks)


def rwkv_layer(h, s, mu, w_rkv, w0, w1, w2, a0, a1, a2, g1, g2, k_k, k_a, r_k, gn_g, gn_b,
               w_out, batch, seq):
    xm = rwkv_mix(h, mu, seq)
    r = matmul_w32(xm, w_rkv, BF16, xi=0, wi=(s, 0))
    k = matmul_w32(xm, w_rkv, BF16, xi=1, wi=(s, 1))
    v = matmul_w32(xm, w_rkv, BF16, xi=2, wi=(s, 2))
    lw = lora(xm, 3, w1, w2, w0, "tanh", "logdecay", F32)
    a = lora(xm, 4, a1, a2, a0, "none", "sigmoid", F32)
    g = lora(xm, 5, g1, g2, jnp.zeros_like(a0), "sigmoid", "none", BF16)
    y = rwkv_core(r, k, v, lw, a, g, k_k, k_a, r_k, gn_g, gn_b, batch, seq)
    return matmul_w32(y, w_out, F32, wi=s)


def _lru_kernel(gate_ref, rec_ref, cw_ref, cb_ref, wgx_ref, bgx_ref, wga_ref, bga_ref,
                lam_ref, o_ref, tail_ref, h_ref):
    t = pl.program_id(2)
    c = rec_ref.shape[0]

    @pl.when(t == 0)
    def _():
        tail_ref[...] = jnp.zeros_like(tail_ref)
        h_ref[...] = jnp.zeros_like(h_ref)

    x = rec_ref[...].astype(F32)
    u = _causal_conv(x, tail_ref[...], cw_ref[...]) + cb_ref[...]
    tail_ref[...] = x[c - SUBLANES:]
    ub = u.astype(BF16)
    i_t = _sigmoid(jnp.dot(ub, wgx_ref[...], preferred_element_type=F32) + bgx_ref[...])
    r_t = _sigmoid(jnp.dot(ub, wga_ref[...], preferred_element_type=F32) + bga_ref[...])
    log_a = -LRU_C * r_t * _softplus(-lam_ref[...])
    a = jnp.exp(log_a)
    th = jnp.tanh(log_a)
    b = jnp.sqrt(-2.0 * th / (1.0 - th)) * (i_t * u)

    row = lax.broadcasted_iota(jnp.int32, a.shape, 0)
    sft = 1
    while sft < c:
        keep = row >= sft
        a_s = jnp.where(keep, pltpu.roll(a, sft, axis=0), 1.0)
        b_s = jnp.where(keep, pltpu.roll(b, sft, axis=0), 0.0)
        b = a * b_s + b
        a = a * a_s
        sft *= 2
    hcur = b + a * h_ref[0:1, :]
    h_ref[0:1, :] = hcur[c - 1:c, :]

    gt = gate_ref[...].astype(F32)
    gelu = 0.5 * gt * (1.0 + jnp.tanh(math.sqrt(2.0 / math.pi) * (gt + 0.044715 * gt * gt * gt)))
    o_ref[...] = (hcur * gelu).astype(o_ref.dtype)


def lru_core(proj, conv_w, conv_b, w_gx, b_gx, w_ga, b_ga, lam, batch, seq):
    m = proj.shape[0]
    c = _pick(seq, LRU_TBLOCK)
    nt = seq // c
    nb = LRU_BLOCKS
    w = LRU_BLOCK
    vec = pl.BlockSpec((1, w), lambda b, j, t: (0, j))
    blk = pl.BlockSpec((None, w, w), lambda b, j, t: (j, 0, 0))
    row = lambda x: x.reshape(1, nb * w).astype(F32)
    return pl.pallas_call(
        _lru_kernel,
        out_shape=jax.ShapeDtypeStruct((m, nb * w), BF16),
        grid=(batch, nb, nt),
        in_specs=[pl.BlockSpec((c, w), lambda b, j, t: (b * nt + t, j)),
                  pl.BlockSpec((c, w), lambda b, j, t: (b * nt + t, nb + j)),
                  pl.BlockSpec((CONV_K, w), lambda b, j, t: (0, j)),
                  vec, blk, vec, blk, vec, vec],
        out_specs=pl.BlockSpec((c, w), lambda b, j, t: (b * nt + t, j)),
        scratch_shapes=[pltpu.VMEM((SUBLANES, w), F32), pltpu.VMEM((SUBLANES, w), F32)],
        compiler_params=_params(("parallel", "parallel", "arbitrary")),
        name="lru_core",
    )(proj, proj, conv_w, row(conv_b), w_gx.astype(BF16), row(b_gx), w_ga.astype(BF16),
      row(b_ga), row(lam))


def lru_layer(hb, s, w_in, conv_w, conv_b, w_gx, b_gx, w_ga, b_ga, lam, w_out, batch, seq):
    proj = matmul_w32(hb, w_in, BF16, wi=s)
    y = lru_core(proj, conv_w, conv_b, w_gx, b_gx, w_ga, b_ga, lam, batch, seq)
    return matmul_w32(y, w_out, F32, wi=s)


def dense_ffn(hb, f, w_gu, w_down):
    act = swiglu_up(hb, w_gu, f)
    return matmul(act, w_down.astype(BF16), F32)


def moe_ffn(h, hb, f, router_w, router_b, w_gu, w_down):
    comb = router(h, router_w, router_b)
    act = moe_up(hb, w_gu, f, comb)
    ne, fe, d = w_down.shape
    return matmul(act, w_down.astype(BF16).reshape(ne * fe, d), F32)


def kernel(x, ln_g, ln_b, gdn_w_in, gdn_conv_w, gdn_a_log, gdn_dt_bias, gdn_norm_w, gdn_w_out, rwkv_mu, rwkv_w_rkv, rwkv_w0, rwkv_w1, rwkv_w2, rwkv_a0, rwkv_a1, rwkv_a2, rwkv_g1, rwkv_g2, rwkv_k_k, rwkv_k_a, rwkv_r_k, rwkv_gn_g, rwkv_gn_b, rwkv_w_out, lru_w_in, lru_conv_w, lru_conv_b, lru_w_gx, lru_b_gx, lru_w_ga, lru_b_ga, lru_lam, lru_w_out, ffn_w_gu, ffn_w_down, moe_router_w, moe_router_b, moe_w_gu, moe_w_down):
    batch, seq, d = x.shape
    h = x.reshape(batch * seq, d)
    hb = h.astype(BF16)
    for i in range(DEPTH):
        m, s = i % 3, i // 3
        if m == 0:
            y = gdn_layer(hb, s, gdn_w_in, gdn_conv_w[s], gdn_a_log[s], gdn_dt_bias[s],
                          gdn_norm_w[s], gdn_w_out, batch, seq)
        elif m == 1:
            y = rwkv_layer(h, s, rwkv_mu[s], rwkv_w_rkv, rwkv_w0[s], rwkv_w1[s], rwkv_w2[s],
                           rwkv_a0[s], rwkv_a1[s], rwkv_a2[s], rwkv_g1[s], rwkv_g2[s],
                           rwkv_k_k[s], rwkv_k_a[s], rwkv_r_k[s], rwkv_gn_g[s], rwkv_gn_b[s],
                           rwkv_w_out, batch, seq)
        else:
            y = lru_layer(hb, s, lru_w_in, lru_conv_w[s], lru_conv_b[s], lru_w_gx[s],
                          lru_b_gx[s], lru_w_ga[s], lru_b_ga[s], lru_lam[s], lru_w_out,
                          batch, seq)
        h, hb = ln_residual(h, y, ln_g[i, 0], ln_b[i, 0])
        f = i // 2
        if i % 2 == 0:
            y = dense_ffn(hb, f, ffn_w_gu, ffn_w_down[f])
        else:
            y = moe_ffn(h, hb, f, moe_router_w[f], moe_router_b[f], moe_w_gu, moe_w_down[f])
        h, hb = ln_residual(h, y, ln_g[i, 1], ln_b[i, 1])
    return h.reshape(batch, seq, d)
```

```python
import functools
import math

import jax
import jax.numpy as jnp
import numpy as np
from jax import lax
from jax.experimental import pallas as pl
from jax.experimental.pallas import tpu as pltpu

F32 = jnp.float32
BF16 = jnp.bfloat16

D_MODEL = 4096
DEPTH = 4
DN_ALPHA = (2 * DEPTH) ** 0.25
LN_EPS = 1e-5
CONV_K = 4
GDN_DK = 128
GDN_HEADS = D_MODEL // GDN_DK
RWKV_HEAD = 64
RWKV_GN_EPS = 64e-5
LRU_BLOCK = 256
LRU_BLOCKS = D_MODEL // LRU_BLOCK
LRU_C = 8.0
N_EXPERTS = 8

V7X_VMEM_BYTES = 64 * 1024 * 1024
VMEM_LIMIT = V7X_VMEM_BYTES - 8 * 1024 * 1024
LANES = 128
SUBLANES = 8

GDN_CHUNK = 256
RWKV_CHUNK = 64
LRU_TBLOCK = 256
INV_BASE = 16
GDN_HEADS_PER_STEP = 4
RWKV_PAIRS_PER_STEP = 16


def _params(sem):
    return pltpu.CompilerParams(dimension_semantics=sem, vmem_limit_bytes=VMEM_LIMIT)


def _dot(a, b):
    return jnp.dot(a.astype(BF16), b.astype(BF16), preferred_element_type=F32)


def _dot_nt(a, b):
    return lax.dot_general(a.astype(BF16), b.astype(BF16), (((1,), (1,)), ((), ())),
                           preferred_element_type=F32)


def _dot_tn(a, b):
    return lax.dot_general(a.astype(BF16), b.astype(BF16), (((0,), (0,)), ((), ())),
                           preferred_element_type=F32)


def _sigmoid(x):
    return 1.0 / (1.0 + jnp.exp(-x))


def _silu(x):
    return x * _sigmoid(x)


def _softplus(x):
    return jnp.maximum(x, 0.0) + jnp.log1p(jnp.exp(-jnp.abs(x)))


def _pick(n, pref):
    t = min(n, pref)
    while n % t:
        t //= 2
    return t


def _mm_kernel(x_ref, w_ref, o_ref, *, nk):
    acc = jnp.dot(x_ref[...], w_ref[...], preferred_element_type=F32)
    if nk == 1:
        o_ref[...] = acc.astype(o_ref.dtype)
    else:
        k = pl.program_id(2)

        @pl.when(k == 0)
        def _():
            o_ref[...] = acc

        @pl.when(k > 0)
        def _():
            o_ref[...] += acc


def matmul(x, w, out_dtype, xi=None, wi=None, n=None, bm=1024, bn=1024, bk=4096):
    m, kdim = x.shape[-2:]
    n = w.shape[-1] if n is None else n
    bm, bn, bk = _pick(m, bm), _pick(n, bn), _pick(kdim, bk)
    nk = kdim // bk
    if nk > 1:
        assert out_dtype == F32
    if xi is None:
        x_spec = pl.BlockSpec((bm, bk), lambda i, j, k: (i, k))
    else:
        x_spec = pl.BlockSpec((None, bm, bk), lambda i, j, k: (xi, i, k))
    if wi is None:
        w_spec = pl.BlockSpec((bk, bn), lambda i, j, k: (k, j))
    else:
        w_spec = pl.BlockSpec((None, bk, bn), lambda i, j, k: (wi, k, j))
    return pl.pallas_call(
        functools.partial(_mm_kernel, nk=nk),
        out_shape=jax.ShapeDtypeStruct((m, n), out_dtype),
        grid=(m // bm, n // bn, nk),
        in_specs=[x_spec, w_spec],
        out_specs=pl.BlockSpec((bm, bn), lambda i, j, k: (i, j)),
        compiler_params=_params(("parallel", "parallel", "arbitrary")),
        name="matmul",
    )(x, w)


def _mm_w32_kernel(x_ref, w_ref, o_ref, wb_ref):
    @pl.when(pl.program_id(1) == 0)
    def _():
        wb_ref[...] = w_ref[...].astype(BF16)

    o_ref[...] = jnp.dot(x_ref[...], wb_ref[...],
                         preferred_element_type=F32).astype(o_ref.dtype)


def matmul_w32(x, w, out_dtype, xi=None, wi=None, n=None, bm=512, bn=1024):
    m, kdim = x.shape[-2:]
    n = w.shape[-1] if n is None else n
    bm, bn = _pick(m, bm), _pick(n, bn)
    if xi is None:
        x_spec = pl.BlockSpec((bm, kdim), lambda j, i: (i, 0))
    else:
        x_spec = pl.BlockSpec((None, bm, kdim), lambda j, i: (xi, i, 0))
    wi = () if wi is None else (wi if isinstance(wi, tuple) else (wi,))
    w_spec = pl.BlockSpec((None,) * len(wi) + (kdim, bn), lambda j, i: wi + (0, j))
    return pl.pallas_call(
        _mm_w32_kernel,
        out_shape=jax.ShapeDtypeStruct((m, n), out_dtype),
        grid=(n // bn, m // bm),
        in_specs=[x_spec, w_spec],
        out_specs=pl.BlockSpec((bm, bn), lambda j, i: (i, j)),
        scratch_shapes=[pltpu.VMEM((kdim, bn), BF16)],
        compiler_params=_params(("parallel", "arbitrary")),
        name="matmul_w32",
    )(x, w)


def _swiglu_kernel(x_ref, wg_ref, wu_ref, o_ref, wgb_ref, wub_ref):
    @pl.when(pl.program_id(1) == 0)
    def _():
        wgb_ref[...] = wg_ref[...].astype(BF16)
        wub_ref[...] = wu_ref[...].astype(BF16)

    x = x_ref[...]
    g = jnp.dot(x, wgb_ref[...], preferred_element_type=F32)
    u = jnp.dot(x, wub_ref[...], preferred_element_type=F32)
    o_ref[...] = (_silu(g) * u).astype(o_ref.dtype)


def swiglu_up(x, w_gu, wi, bm=512, bn=512):
    m, d = x.shape
    f = w_gu.shape[2] // 2
    bm, bn = _pick(m, bm), _pick(f, bn)
    nj = f // bn
    return pl.pallas_call(
        _swiglu_kernel,
        out_shape=jax.ShapeDtypeStruct((m, f), BF16),
        grid=(nj, m // bm),
        in_specs=[pl.BlockSpec((bm, d), lambda j, i: (i, 0)),
                  pl.BlockSpec((None, d, bn), lambda j, i: (wi, 0, j)),
                  pl.BlockSpec((None, d, bn), lambda j, i: (wi, 0, j + nj))],
        out_specs=pl.BlockSpec((bm, bn), lambda j, i: (i, j)),
        scratch_shapes=[pltpu.VMEM((d, bn), BF16), pltpu.VMEM((d, bn), BF16)],
        compiler_params=_params(("parallel", "arbitrary")),
        name="swiglu_up",
    )(x, w_gu, w_gu)


def _moe_up_kernel(x_ref, wg_ref, wu_ref, comb_ref, o_ref, wgb_ref, wub_ref):
    e = pl.program_id(0)

    @pl.when(pl.program_id(2) == 0)
    def _():
        wgb_ref[...] = wg_ref[...].astype(BF16)
        wub_ref[...] = wu_ref[...].astype(BF16)

    x = x_ref[...]
    g = jnp.dot(x, wgb_ref[...], preferred_element_type=F32)
    u = jnp.dot(x, wub_ref[...], preferred_element_type=F32)
    comb = comb_ref[...]
    lane = lax.broadcasted_iota(jnp.int32, comb.shape, 1)
    ce = jnp.sum(jnp.where(lane == e, comb, 0.0), axis=1, keepdims=True)
    o_ref[...] = (_silu(g) * u * ce).astype(o_ref.dtype)


def moe_up(x, w_gu, li, comb, bm=512, bn=512):
    m, d = x.shape
    ne = N_EXPERTS
    f = w_gu.shape[3] // 2
    bm, bn = _pick(m, bm), _pick(f, bn)
    nj = f // bn
    return pl.pallas_call(
        _moe_up_kernel,
        out_shape=jax.ShapeDtypeStruct((m, ne * f), BF16),
        grid=(ne, nj, m // bm),
        in_specs=[pl.BlockSpec((bm, d), lambda e, j, i: (i, 0)),
                  pl.BlockSpec((None, None, d, bn), lambda e, j, i: (li, e, 0, j)),
                  pl.BlockSpec((None, None, d, bn), lambda e, j, i: (li, e, 0, j + nj)),
                  pl.BlockSpec((bm, LANES), lambda e, j, i: (i, 0))],
        out_specs=pl.BlockSpec((bm, bn), lambda e, j, i: (i, e * nj + j)),
        scratch_shapes=[pltpu.VMEM((d, bn), BF16), pltpu.VMEM((d, bn), BF16)],
        compiler_params=_params(("parallel", "parallel", "arbitrary")),
        name="moe_up",
    )(x, w_gu, w_gu, comb)


def _router_kernel(x_ref, w_ref, b_ref, o_ref):
    logits = jnp.dot(x_ref[...], w_ref[...], preferred_element_type=F32,
                     precision=lax.Precision.HIGHEST) + b_ref[...]
    lane = lax.broadcasted_iota(jnp.int32, logits.shape, 1)
    neg = jnp.float32(-jnp.inf)
    logits = jnp.where(lane < N_EXPERTS, logits, neg)
    m1 = jnp.max(logits, axis=1, keepdims=True)
    i1 = jnp.min(jnp.where(logits == m1, lane, LANES), axis=1, keepdims=True)
    sel1 = lane == i1
    rest = jnp.where(sel1, neg, logits)
    m2 = jnp.max(rest, axis=1, keepdims=True)
    i2 = jnp.min(jnp.where(rest == m2, lane, LANES), axis=1, keepdims=True)
    sel2 = lane == i2
    e2 = jnp.exp(m2 - m1)
    p1 = 1.0 / (1.0 + e2)
    p2 = e2 / (1.0 + e2)
    o_ref[...] = jnp.where(sel1, p1, 0.0) + jnp.where(sel2, p2, 0.0)


def router(x, rw, rb, bm=512):
    m, d = x.shape
    bm = _pick(m, bm)
    rw_p = jnp.zeros((d, LANES), F32).at[:, :N_EXPERTS].set(rw.astype(F32))
    rb_p = jnp.zeros((1, LANES), F32).at[0, :N_EXPERTS].set(rb.astype(F32))
    return pl.pallas_call(
        _router_kernel,
        out_shape=jax.ShapeDtypeStruct((m, LANES), F32),
        grid=(m // bm,),
        in_specs=[pl.BlockSpec((bm, d), lambda i: (i, 0)),
                  pl.BlockSpec((d, LANES), lambda i: (0, 0)),
                  pl.BlockSpec((1, LANES), lambda i: (0, 0))],
        out_specs=pl.BlockSpec((bm, LANES), lambda i: (i, 0)),
        compiler_params=_params(("parallel",)),
        name="router",
    )(x, rw_p, rb_p)


def _ln_kernel(h_ref, y_ref, g_ref, b_ref, o_ref, ob_ref):
    x = DN_ALPHA * h_ref[...] + y_ref[...]
    mu = jnp.mean(x, axis=-1, keepdims=True)
    xc = x - mu
    var = jnp.mean(xc * xc, axis=-1, keepdims=True)
    o = xc * lax.rsqrt(var + LN_EPS) * g_ref[...] + b_ref[...]
    o_ref[...] = o
    ob_ref[...] = o.astype(BF16)


def ln_residual(h, y, g, b, tm=256):
    m, d = h.shape
    tm = _pick(m, tm)
    row = pl.BlockSpec((tm, d), lambda i: (i, 0))
    vec = pl.BlockSpec((1, d), lambda i: (0, 0))
    return pl.pallas_call(
        _ln_kernel,
        out_shape=(jax.ShapeDtypeStruct((m, d), F32), jax.ShapeDtypeStruct((m, d), BF16)),
        grid=(m // tm,),
        in_specs=[row, row, vec, vec],
        out_specs=(row, row),
        compiler_params=_params(("parallel",)),
        name="ln_residual",
    )(h, y, g.reshape(1, d), b.reshape(1, d))


def _causal_conv(x, tail, w):
    c = x.shape[0]
    xs = jnp.concatenate([tail, x], axis=0)
    y = xs[SUBLANES:] * w[CONV_K - 1:CONV_K]
    for j in range(CONV_K - 1):
        y = y + pltpu.roll(xs, CONV_K - 1 - j, axis=0)[SUBLANES:SUBLANES + c] * w[j:j + 1]
    return y


def _tri_masks(n, period, top):
    i = np.arange(n)[:, None]
    j = np.arange(n)[None, :]
    ms = [i == j, (i // INV_BASE) == (j // INV_BASE)]
    size = INV_BASE
    while size < top:
        bi, bj = i // size, j // size
        ms.append((bi // 2 == bj // 2) & (bi % 2 == 1) & (bj % 2 == 0))
        size *= 2
    ti, tj = i % period, j % period
    ms += [ti >= tj, ti > tj, ti <= tj]
    return jnp.asarray(np.stack(ms).astype(np.float32))


def _n_merge_levels(top):
    return (top // INV_BASE).bit_length() - 1


def _unit_lower_inverse(a_strict, m_ref, top):
    ps = [-a * m_ref[1] for a in a_strict]
    xs = [m_ref[0] + p for p in ps]
    k = 2
    while k < INV_BASE:
        ps = [_dot(p, p) for p in ps]
        xs = [x + _dot(x, p) for x, p in zip(xs, ps)]
        k *= 2
    for lvl in range(_n_merge_levels(top)):
        ys = [_dot(a * m_ref[2 + lvl], x) for a, x in zip(a_strict, xs)]
        xs = [x - _dot(x, y) for x, y in zip(xs, ys)]
    return xs


def _gdn_kernel(alog_ref, dtb_ref, q_ref, k_ref, v_ref, z_ref, ba_ref,
                cwq_ref, cwk_ref, cwv_ref, nw_ref, m_ref, o_ref, s_ref, tail_ref, *, heads):
    hg = pl.program_id(1)
    t = pl.program_id(2)
    c = q_ref.shape[0]
    nl = _n_merge_levels(c)
    causal, strict, upper = m_ref[2 + nl], m_ref[3 + nl], m_ref[4 + nl]
    eye = m_ref[0]

    @pl.when(t == 0)
    def _():
        s_ref[...] = jnp.zeros_like(s_ref)
        tail_ref[...] = jnp.zeros_like(tail_ref)

    ba = ba_ref[...]
    lane = lax.broadcasted_iota(jnp.int32, ba.shape, 1)

    hs = range(heads)
    sls = [slice(gi * GDN_DK, (gi + 1) * GDN_DK) for gi in hs]

    def conv_silu(x_ref, cw_ref, slot, sl):
        x = x_ref[:, sl].astype(F32)
        y = _causal_conv(x, tail_ref[slot, :, sl], cw_ref[:, sl])
        tail_ref[slot, :, sl] = x[c - SUBLANES:]
        return _silu(y)

    qs = [conv_silu(q_ref, cwq_ref, 0, sl) for sl in sls]
    ks = [conv_silu(k_ref, cwk_ref, 1, sl) for sl in sls]
    vs = [conv_silu(v_ref, cwv_ref, 2, sl) for sl in sls]
    qs = [q * lax.rsqrt(jnp.sum(q * q, axis=-1, keepdims=True) + 1e-6) * (GDN_DK ** -0.5)
          for q in qs]
    ks = [k * lax.rsqrt(jnp.sum(k * k, axis=-1, keepdims=True) + 1e-6) for k in ks]

    betas, g_cols, gammas = [], [], []
    for gi in hs:
        h = hg * heads + gi
        b_col = jnp.sum(jnp.where(lane == h, ba, 0.0), axis=1, keepdims=True)
        a_col = jnp.sum(jnp.where(lane == h + GDN_HEADS, ba, 0.0), axis=1, keepdims=True)
        betas.append(_sigmoid(b_col))
        g = -jnp.exp(alog_ref[h]) * _softplus(a_col + dtb_ref[h])
        g_row = jnp.sum(g * upper, axis=0, keepdims=True)
        g_col = jnp.sum(g_row * eye, axis=1, keepdims=True)
        g_cols.append(g_col)
        gammas.append(jnp.exp((g_col - g_row) * causal) * causal)

    kbs = [k * b for k, b in zip(ks, betas)]
    a_mats = [_dot_nt(kb, k) * (gm * strict) for kb, k, gm in zip(kbs, ks, gammas)]
    qks = [_dot_nt(q, k) * gm for q, k, gm in zip(qs, ks, gammas)]
    t_invs = _unit_lower_inverse(a_mats, m_ref, c)

    egs = [jnp.exp(gc) for gc in g_cols]
    uws = [_dot(ti, jnp.concatenate([v * b, kb * eg], axis=1))
           for ti, v, b, kb, eg in zip(t_invs, vs, betas, kbs, egs)]
    ss = [s_ref[gi] for gi in hs]
    ws_qs = [_dot(jnp.concatenate([uw[:, GDN_DK:], q * eg], axis=0), s)
             for uw, q, eg, s in zip(uws, qs, egs, ss)]
    v_news = [uw[:, :GDN_DK] - wq[:c] for uw, wq in zip(uws, ws_qs)]
    os_ = [wq[c:] + _dot(qk, vn) for wq, qk, vn in zip(ws_qs, qks, v_news)]
    for gi in hs:
        g_last = g_cols[gi][c - 1:c, :]
        s_ref[gi] = (ss[gi] * jnp.exp(g_last)
                     + _dot_tn(ks[gi] * jnp.exp(g_last - g_cols[gi]), v_news[gi]))
    for gi in hs:
        o = os_[gi]
        o = o * lax.rsqrt(jnp.mean(o * o, axis=-1, keepdims=True) + 1e-6) * nw_ref[...]
        o_ref[:, sls[gi]] = (o * _silu(z_ref[:, sls[gi]].astype(F32))).astype(o_ref.dtype)


def gdn_core(qkvz, ba, conv_w, a_log, dt_bias, norm_w, batch, seq):
    m = qkvz.shape[0]
    c = _pick(seq, GDN_CHUNK)
    nt = seq // c
    nh = GDN_HEADS
    g = GDN_HEADS_PER_STEP
    ng = nh // g
    w = g * GDN_DK
    masks = _tri_masks(c, c, c)

    def col(off):
        return pl.BlockSpec((c, w), lambda b, h, t: (b * nt + t, off * ng + h))

    def cw(off):
        return pl.BlockSpec((CONV_K, w), lambda b, h, t: (0, off * ng + h))

    smem = pl.BlockSpec(memory_space=pltpu.SMEM)
    return pl.pallas_call(
        functools.partial(_gdn_kernel, heads=g),
        out_shape=jax.ShapeDtypeStruct((m, D_MODEL), BF16),
        grid=(batch, ng, nt),
        in_specs=[smem, smem, col(0), col(1), col(2), col(3),
                  pl.BlockSpec((c, 2 * nh), lambda b, h, t: (b * nt + t, 0)),
                  cw(0), cw(1), cw(2),
                  pl.BlockSpec((1, GDN_DK), lambda b, h, t: (0, 0)),
                  pl.BlockSpec(masks.shape, lambda b, h, t: (0, 0, 0))],
        out_specs=pl.BlockSpec((c, w), lambda b, h, t: (b * nt + t, h)),
        scratch_shapes=[pltpu.VMEM((g, GDN_DK, GDN_DK), F32),
                        pltpu.VMEM((3, SUBLANES, w), F32)],
        compiler_params=_params(("parallel", "parallel", "arbitrary")),
        name="gdn_core",
    )(a_log.astype(F32), dt_bias.astype(F32), qkvz, qkvz, qkvz, qkvz, ba,
      conv_w, conv_w, conv_w, norm_w.reshape(1, GDN_DK), masks)


def gdn_layer(hb, s, w_in, conv_w, a_log, dt_bias, norm_w, w_out, batch, seq):
    d4 = 4 * D_MODEL
    qkvz = matmul(hb, w_in[s, :, :d4].astype(BF16), BF16)
    ba = matmul(hb, w_in[s, :, d4:].astype(BF16), F32)
    o = gdn_core(qkvz, ba, conv_w, a_log, dt_bias, norm_w, batch, seq)
    return matmul_w32(o, w_out, F32, wi=s)


def _rwkv_mix_kernel(x_ref, halo_ref, mu_ref, o_ref, *, seq):
    i = pl.program_id(0)
    tm = x_ref.shape[0]
    x = x_ref[...]
    prev = halo_ref[SUBLANES - 1:SUBLANES, :]
    prev = jnp.where((i * tm) % seq == 0, jnp.zeros_like(prev), prev)
    row = lax.broadcasted_iota(jnp.int32, x.shape, 0)
    xprev = jnp.where(row == 0, prev, pltpu.roll(x, 1, axis=0))
    xx = xprev - x
    for p in range(6):
        o_ref[p] = (x + xx * mu_ref[p:p + 1, :]).astype(o_ref.dtype)


def rwkv_mix(h, mu, seq, tm=256):
    m, d = h.shape
    tm = _pick(seq, tm)
    r8 = tm // SUBLANES
    return pl.pallas_call(
        functools.partial(_rwkv_mix_kernel, seq=seq),
        out_shape=jax.ShapeDtypeStruct((6, m, d), BF16),
        grid=(m // tm,),
        in_specs=[pl.BlockSpec((tm, d), lambda i: (i, 0)),
                  pl.BlockSpec((SUBLANES, d), lambda i: (jnp.maximum(i * r8 - 1, 0), 0)),
                  pl.BlockSpec((6, d), lambda i: (0, 0))],
        out_specs=pl.BlockSpec((6, tm, d), lambda i: (0, i, 0)),
        compiler_params=_params(("parallel",)),
        name="rwkv_mix",
    )(h, h, mu)


def _lora_kernel(x_ref, a_ref, b_ref, bias_ref, o_ref, *, act, epi):
    t = jnp.dot(x_ref[...], a_ref[...], preferred_element_type=F32)
    if act == "tanh":
        t = jnp.tanh(t)
    elif act == "sigmoid":
        t = _sigmoid(t)
    y = jnp.dot(t.astype(BF16), b_ref[...], preferred_element_type=F32) + bias_ref[...]
    if epi == "logdecay":
        y = -jnp.exp(-_softplus(-y) - 0.5)
    elif epi == "sigmoid":
        y = _sigmoid(y)
    o_ref[...] = y.astype(o_ref.dtype)


def lora(x, xi, a, b, bias, act, epi, out_dtype, tm=512):
    _, m, d = x.shape
    r = a.shape[1]
    n = b.shape[1]
    tm = _pick(m, tm)
    return pl.pallas_call(
        functools.partial(_lora_kernel, act=act, epi=epi),
        out_shape=jax.ShapeDtypeStruct((m, n), out_dtype),
        grid=(m // tm,),
        in_specs=[pl.BlockSpec((None, tm, d), lambda i: (xi, i, 0)),
                  pl.BlockSpec((d, r), lambda i: (0, 0)),
                  pl.BlockSpec((r, n), lambda i: (0, 0)),
                  pl.BlockSpec((1, n), lambda i: (0, 0))],
        out_specs=pl.BlockSpec((tm, n), lambda i: (i, 0)),
        compiler_params=_params(("parallel",)),
        name="lora_" + epi,
    )(x, a.astype(BF16), b.astype(BF16), bias.reshape(1, n).astype(F32))


def _group_sum(x, lo):
    s0 = jnp.sum(jnp.where(lo, x, 0.0), axis=1, keepdims=True)
    s1 = jnp.sum(jnp.where(lo, 0.0, x), axis=1, keepdims=True)
    return jnp.where(lo, s0, s1)


def _rwkv_kernel(r_ref, k_ref, v_ref, lw_ref, a_ref, g_ref, kk_ref, ka_ref, rk_ref,
                 gng_ref, gnb_ref, m_ref, o_ref, s_ref, *, pairs):
    t = pl.program_id(2)
    c = r_ref.shape[0]
    c2 = 2 * c
    nl = _n_merge_levels(c)
    incl, strict = m_ref[2 + nl], m_ref[3 + nl]

    @pl.when(t == 0)
    def _():
        s_ref[...] = jnp.zeros_like(s_ref)

    lane = lax.broadcasted_iota(jnp.int32, (c, LANES), 1)
    row = lax.broadcasted_iota(jnp.int32, (c, LANES), 0)
    lo = lane < RWKV_HEAD

    def stack(x):
        return jnp.concatenate([jnp.where(lo, x, 0.0), jnp.where(lo, 0.0, x)], axis=0)

    ps = range(pairs)
    sls = [slice(pi * LANES, (pi + 1) * LANES) for pi in ps]
    rs = [r_ref[:, sl].astype(F32) for sl in sls]
    ks = [k_ref[:, sl].astype(F32) for sl in sls]
    vs = [v_ref[:, sl].astype(F32) for sl in sls]
    lws = [lw_ref[:, sl] for sl in sls]
    a_s = [a_ref[:, sl] for sl in sls]

    kks = [k * kk_ref[:, sl] for k, sl in zip(ks, sls)]
    kks = [kk * lax.rsqrt(_group_sum(kk * kk, lo) + 1e-6) for kk in kks]
    k2s = [k * (1.0 + (a - 1.0) * ka_ref[:, sl]) for k, a, sl in zip(ks, a_s, sls)]
    kkas = [kk * a for kk, a in zip(kks, a_s)]

    cls = lws
    sft = 1
    while sft < c:
        cls = [cl + jnp.where(row >= sft, pltpu.roll(cl, sft, axis=0), 0.0) for cl in cls]
        sft *= 2

    krs = [jnp.concatenate([stack(jnp.exp(cl - lw) * kk), stack(jnp.exp(cl) * r)], axis=0)
           for cl, lw, kk, r in zip(cls, lws, kks, rs)]
    aks = [jnp.concatenate([stack(jnp.exp(-cl) * kka), stack(jnp.exp(-cl) * k2)], axis=0)
           for cl, kka, k2 in zip(cls, kkas, k2s)]
    vss = [stack(v) for v in vs]
    scs = [_dot_nt(kr, ak) for kr, ak in zip(krs, aks)]
    t_invs = _unit_lower_inverse([sc[:c2, :c2] * strict for sc in scs], m_ref, c)

    ss = [s_ref[pi] for pi in ps]
    kz_rzs = [_dot_nt(kr, s) for kr, s in zip(krs, ss)]
    avs = [_dot(sc[:c2, c2:] * strict, v) for sc, v in zip(scs, vss)]
    us = [_dot(ti, -kz[:c2] - av) for ti, kz, av in zip(t_invs, kz_rzs, avs)]
    uvs = [jnp.concatenate([u, v], axis=0) for u, v in zip(us, vss)]
    yss = [kz[c2:] + _dot(sc[c2:, :] * jnp.concatenate([incl, incl], axis=1), uv)
           for kz, sc, uv in zip(kz_rzs, scs, uvs)]
    for pi in ps:
        cl_last = cls[pi][c - 1:c, :]
        dec = jnp.exp(cl_last - cls[pi])
        ak2 = jnp.concatenate([stack(dec * kkas[pi]), stack(dec * k2s[pi])], axis=0)
        s_ref[pi] = ss[pi] * jnp.exp(cl_last) + _dot_tn(uvs[pi], ak2)

    inv_n = 1.0 / RWKV_HEAD
    for pi in ps:
        sl = sls[pi]
        y = yss[pi][:c] + yss[pi][c:]
        mean = _group_sum(y, lo) * inv_n
        yc = y - mean
        var = _group_sum(yc * yc, lo) * inv_n
        yn = yc * lax.rsqrt(var + RWKV_GN_EPS) * gng_ref[:, sl] + gnb_ref[:, sl]
        bonus = _group_sum(rs[pi] * k2s[pi] * rk_ref[:, sl], lo) * vs[pi]
        o_ref[:, sl] = ((yn + bonus) * g_ref[:, sl].astype(F32)).astype(o_ref.dtype)


def rwkv_core(r, k, v, lw, a, g, k_k, k_a, r_k, gn_g, gn_b, batch, seq):
    m, d = r.shape
    c = _pick(seq, RWKV_CHUNK)
    nt = seq // c
    pairs = RWKV_PAIRS_PER_STEP
    w = pairs * LANES
    masks = _tri_masks(2 * c, c, c)
    tok = pl.BlockSpec((c, w), lambda b, p, t: (b * nt + t, p))
    vec = pl.BlockSpec((1, w), lambda b, p, t: (0, p))
    row = lambda x: x.reshape(1, d).astype(F32)
    return pl.pallas_call(
        functools.partial(_rwkv_kernel, pairs=pairs),
        out_shape=jax.ShapeDtypeStruct((m, d), BF16),
        grid=(batch, d // w, nt),
        in_specs=[tok] * 6 + [vec] * 5 + [pl.BlockSpec(masks.shape, lambda b, p, t: (0, 0, 0))],
        out_specs=tok,
        scratch_shapes=[pltpu.VMEM((pairs, LANES, LANES), F32)],
        compiler_params=_params(("parallel", "parallel", "arbitrary")),
        name="rwkv_core",
    )(r, k, v, lw, a, g, row(k_k), row(k_a), row(r_k), row(gn_g), row(gn_b), masks)


def rwkv_layer(h, s, mu, w_rkv, w0, w1, w2, a0, a1, a2, g1, g2, k_k, k_a, r_k, gn_g, gn_b,
               w_out, batch, seq):
    xm = rwkv_mix(h, mu, seq)
    r = matmul_w32(xm, w_rkv, BF16, xi=0, wi=(s, 0))
    k = matmul_w32(xm, w_rkv, BF16, xi=1, wi=(s, 1))
    v = matmul_w32(xm, w_rkv, BF16, xi=2, wi=(s, 2))
    lw = lora(xm, 3, w1, w2, w0, "tanh", "logdecay", F32)
    a = lora(xm, 4, a1, a2, a0, "none", "sigmoid", F32)
    g = lora(xm, 5, g1, g2, jnp.zeros_like(a0), "sigmoid", "none", BF16)
    y = rwkv_core(r, k, v, lw, a, g, k_k, k_a, r_k, gn_g, gn_b, batch, seq)
    return matmul_w32(y, w_out, F32, wi=s)


def _lru_kernel(gate_ref, rec_ref, cw_ref, cb_ref, wgx_ref, bgx_ref, wga_ref, bga_ref,
                lam_ref, o_ref, tail_ref, h_ref):
    t = pl.program_id(2)
    c = rec_ref.shape[0]

    @pl.when(t == 0)
    def _():
        tail_ref[...] = jnp.zeros_like(tail_ref)
        h_ref[...] = jnp.zeros_like(h_ref)

    x = rec_ref[...].astype(F32)
    u = _causal_conv(x, tail_ref[...], cw_ref[...]) + cb_ref[...]
    tail_ref[...] = x[c - SUBLANES:]
    ub = u.astype(BF16)
    i_t = _sigmoid(jnp.dot(ub, wgx_ref[...], preferred_element_type=F32) + bgx_ref[...])
    r_t = _sigmoid(jnp.dot(ub, wga_ref[...], preferred_element_type=F32) + bga_ref[...])
    log_a = -LRU_C * r_t * _softplus(-lam_ref[...])
    a = jnp.exp(log_a)
    th = jnp.tanh(log_a)
    b = jnp.sqrt(-2.0 * th / (1.0 - th)) * (i_t * u)

    row = lax.broadcasted_iota(jnp.int32, a.shape, 0)
    sft = 1
    while sft < c:
        keep = row >= sft
        a_s = jnp.where(keep, pltpu.roll(a, sft, axis=0), 1.0)
        b_s = jnp.where(keep, pltpu.roll(b, sft, axis=0), 0.0)
        b = a * b_s + b
        a = a * a_s
        sft *= 2
    hcur = b + a * h_ref[0:1, :]
    h_ref[0:1, :] = hcur[c - 1:c, :]

    gt = gate_ref[...].astype(F32)
    gelu = 0.5 * gt * (1.0 + jnp.tanh(math.sqrt(2.0 / math.pi) * (gt + 0.044715 * gt * gt * gt)))
    o_ref[...] = (hcur * gelu).astype(o_ref.dtype)


def lru_core(proj, conv_w, conv_b, w_gx, b_gx, w_ga, b_ga, lam, batch, seq):
    m = proj.shape[0]
    c = _pick(seq, LRU_TBLOCK)
    nt = seq // c
    nb = LRU_BLOCKS
    w = LRU_BLOCK
    vec = pl.BlockSpec((1, w), lambda b, j, t: (0, j))
    blk = pl.BlockSpec((None, w, w), lambda b, j, t: (j, 0, 0))
    row = lambda x: x.reshape(1, nb * w).astype(F32)
    return pl.pallas_call(
        _lru_kernel,
        out_shape=jax.ShapeDtypeStruct((m, nb * w), BF16),
        grid=(batch, nb, nt),
        in_specs=[pl.BlockSpec((c, w), lambda b, j, t: (b * nt + t, j)),
                  pl.BlockSpec((c, w), lambda b, j, t: (b * nt + t, nb + j)),
                  pl.BlockSpec((CONV_K, w), lambda b, j, t: (0, j)),
                  vec, blk, vec, blk, vec, vec],
        out_specs=pl.BlockSpec((c, w), lambda b, j, t: (b * nt + t, j)),
        scratch_shapes=[pltpu.VMEM((SUBLANES, w), F32), pltpu.VMEM((SUBLANES, w), F32)],
        compiler_params=_params(("parallel", "parallel", "arbitrary")),
        name="lru_core",
    )(proj, proj, conv_w, row(conv_b), w_gx.astype(BF16), row(b_gx), w_ga.astype(BF16),
      row(b_ga), row(lam))


def lru_layer(hb, s, w_in, conv_w, conv_b, w_gx, b_gx, w_ga, b_ga, lam, w_out, batch, seq):
    proj = matmul_w32(hb, w_in, BF16, wi=s)
    y = lru_core(proj, conv_w, conv_b, w_gx, b_gx, w_ga, b_ga, lam, batch, seq)
    return matmul_w32(y, w_out, F32, wi=s)


def dense_ffn(hb, f, w_gu, w_down):
    act = swiglu_up(hb, w_gu, f)
    return matmul(act, w_down.astype(BF16), F32)


def moe_ffn(h, hb, f, router_w, router_b, w_gu, w_down):
    comb = router(h, router_w, router_b)
    act = moe_up(hb, w_gu, f, comb)
    ne, fe, d = w_down.shape
    return matmul(act, w_down.astype(BF16).reshape(ne * fe, d), F32)


def kernel(x, ln_g, ln_b, gdn_w_in, gdn_conv_w, gdn_a_log, gdn_dt_bias, gdn_norm_w, gdn_w_out, rwkv_mu, rwkv_w_rkv, rwkv_w0, rwkv_w1, rwkv_w2, rwkv_a0, rwkv_a1, rwkv_a2, rwkv_g1, rwkv_g2, rwkv_k_k, rwkv_k_a, rwkv_r_k, rwkv_gn_g, rwkv_gn_b, rwkv_w_out, lru_w_in, lru_conv_w, lru_conv_b, lru_w_gx, lru_b_gx, lru_w_ga, lru_b_ga, lru_lam, lru_w_out, ffn_w_gu, ffn_w_down, moe_router_w, moe_router_b, moe_w_gu, moe_w_down):
    batch, seq, d = x.shape
    h = x.reshape(batch * seq, d)
    hb = h.astype(BF16)
    for i in range(DEPTH):
        m, s = i % 3, i // 3
        if m == 0:
            y = gdn_layer(hb, s, gdn_w_in, gdn_conv_w[s], gdn_a_log[s], gdn_dt_bias[s],
                          gdn_norm_w[s], gdn_w_out, batch, seq)
        elif m == 1:
            y = rwkv_layer(h, s, rwkv_mu[s], rwkv_w_rkv, rwkv_w0[s], rwkv_w1[s], rwkv_w2[s],
                           rwkv_a0[s], rwkv_a1[s], rwkv_a2[s], rwkv_g1[s], rwkv_g2[s],
                           rwkv_k_k[s], rwkv_k_a[s], rwkv_r_k[s], rwkv_gn_g[s], rwkv_gn_b[s],
                           rwkv_w_out, batch, seq)
        else:
            y = lru_layer(hb, s, lru_w_in, lru_conv_w[s], lru_conv_b[s], lru_w_gx[s],
                          lru_b_gx[s], lru_w_ga[s], lru_b_ga[s], lru_lam[s], lru_w_out,
                          batch, seq)
        h, hb = ln_residual(h, y, ln_g[i, 0], ln_b[i, 0])
        f = i // 2
        if i % 2 == 0:
            y = dense_ffn(hb, f, ffn_w_gu, ffn_w_down[f])
        else:
            y = moe_ffn(h, hb, f, moe_router_w[f], moe_router_b[f], moe_w_gu, moe_w_down[f])
        h, hb = ln_residual(h, y, ln_g[i, 1], ln_b[i, 1])
    return h.reshape(batch, seq, d)
```
